```python
import jax
import jax.numpy as jnp
from jax import lax
import numpy as np

D_MODEL = 2048
BATCH = 16
SEQ = 2048
DEPTH = 4

D_MIX = D_MODEL
D_RWKV = D_MIX // 2
D_CONV = D_MIX - D_RWKV
HEAD_DIM = 64
N_HEADS = D_RWKV // HEAD_DIM
DECAY_RANK = 64
ICLR_RANK = 64
VMIX_RANK = 32
GATE_RANK = 160
CONV_WIDTH = 31
D_FF = 4 * D_MODEL
D_PLE = 256
RMS_EPS = 1e-6
GN_EPS = 64e-5
LN_EPS = 1e-5
RWKV_COLS = 3 * D_RWKV + DECAY_RANK + ICLR_RANK + GATE_RANK
D_IN = RWKV_COLS + 2 * D_CONV

kernel_name = "hybrid_rwkv7_conformer_conv_block"


def rms_norm(x, g):
    xf = x.astype(jnp.float32)
    y = xf * lax.rsqrt(jnp.mean(xf * xf, axis=-1, keepdims=True) + RMS_EPS)
    return (y * g.astype(jnp.float32)).astype(x.dtype)


def lerp_token_shift(z, mu):
    prev = jnp.pad(z[:, :-1], ((0, 0), (1, 0), (0, 0)))
    return z + (prev - z) * mu


def wkv7_scan(r, w, k, v, a, b):
    bsz, _, h, n = r.shape

    def step(state, inp):
        r_t, w_t, k_t, v_t, a_t, b_t = inp
        sa = jnp.einsum('bhvk,bhk->bhv', state, a_t)
        state = (state * w_t[:, :, None, :]
                 + sa[..., None] * b_t[:, :, None, :]
                 + v_t[..., None] * k_t[:, :, None, :])
        y_t = jnp.einsum('bhvk,bhk->bhv', state, r_t)
        return state, y_t

    xs = tuple(jnp.swapaxes(t, 0, 1) for t in (r, w, k, v, a, b))
    s0 = jnp.zeros((bsz, h, n, n), jnp.float32)
    _, ys = lax.scan(step, s0, xs)
    return jnp.swapaxes(ys, 0, 1)


def rwkv7_time_mix(z, vmix, v_first, w0, w_decay_up, a0, w_iclr_up, w_gate_up,
                   k_k, k_a, r_k, gn_gain, gn_bias):
    f32 = jnp.float32
    z = z.astype(f32)
    bsz, seq, _ = z.shape
    r, k, v, wd, ad, gd = jnp.split(
        z, [D_RWKV, 2 * D_RWKV, 3 * D_RWKV, 3 * D_RWKV + DECAY_RANK,
            3 * D_RWKV + DECAY_RANK + ICLR_RANK], axis=-1)
    w_raw = w0 + jnp.tanh(wd) @ w_decay_up
    decay = jnp.exp(-jnp.exp(-jax.nn.softplus(-w_raw) - 0.5))
    iclr = jax.nn.sigmoid(a0 + ad @ w_iclr_up)
    gate = jax.nn.sigmoid(gd) @ w_gate_up
    if vmix is None:
        v_first = v
    else:
        vd, v0, w_vmix_up = vmix
        v = v + (v_first - v) * jax.nn.sigmoid(v0 + vd.astype(f32) @ w_vmix_up)

    def heads(t):
        return t.reshape(bsz, seq, N_HEADS, HEAD_DIM)

    kk = heads(k * k_k)
    kk = kk / jnp.maximum(jnp.sqrt(jnp.sum(kk * kk, axis=-1, keepdims=True)), 1e-12)
    k = heads(k * (1.0 + (iclr - 1.0) * k_a))
    r, v, decay, iclr = heads(r), heads(v), heads(decay), heads(iclr)
    y = wkv7_scan(r, decay, k, v, -kk, kk * iclr)
    mu = jnp.mean(y, axis=-1, keepdims=True)
    var = jnp.mean(jnp.square(y - mu), axis=-1, keepdims=True)
    y = ((y - mu) * lax.rsqrt(var + GN_EPS)).reshape(bsz, seq, D_RWKV) * gn_gain + gn_bias
    bonus = jnp.sum(r * k * r_k, axis=-1, keepdims=True) * v
    y = (y + bonus.reshape(bsz, seq, D_RWKV)) * gate
    return y, v_first


def conformer_conv(z, dw_w, dw_b, ln_gain, ln_bias):
    u_lin, u_gate = jnp.split(z, 2, axis=-1)
    u = u_lin * jax.nn.sigmoid(u_gate)
    c = lax.conv_general_dilated(
        u, dw_w[:, None, :].astype(u.dtype), window_strides=(1,),
        padding=((CONV_WIDTH - 1, 0),), dimension_numbers=('NWC', 'WIO', 'NWC'),
        feature_group_count=D_CONV) + dw_b
    cf = c.astype(jnp.float32)
    mu = jnp.mean(cf, axis=-1, keepdims=True)
    var = jnp.mean(jnp.square(cf - mu), axis=-1, keepdims=True)
    cf = (cf - mu) * lax.rsqrt(var + LN_EPS) * ln_gain + ln_bias
    return jax.nn.silu(cf).astype(z.dtype)


def setup_inputs(seed: int = 0) -> dict:
    key = jax.random.key(seed)
    ks = iter(jax.random.split(key, 40))
    f32 = jnp.float32

    def nrm(shape, scale):
        return scale * jax.random.normal(next(ks), shape, f32)

    def uni(shape):
        return jax.random.uniform(next(ks), shape, f32)

    L, Lv = DEPTH, DEPTH - 1
    return {
        'x': nrm((BATCH, SEQ, D_MODEL), 1.0),
        'p': nrm((DEPTH, BATCH, SEQ, D_PLE), 1.0),
        'norm_mix_pre': 1.0 + nrm((L, D_MODEL), 0.05),
        'norm_mix_post': 1.0 + nrm((L, D_MODEL), 0.05),
        'norm_mlp_pre': 1.0 + nrm((L, D_MODEL), 0.05),
        'norm_mlp_post': 1.0 + nrm((L, D_MODEL), 0.05),
        'w_in': nrm((L, D_MODEL, D_IN), D_MODEL ** -0.5),
        'w_in_vmix': nrm((Lv, D_MODEL, VMIX_RANK), D_MODEL ** -0.5),
        'mu_shift': uni((L, RWKV_COLS)),
        'mu_shift_vmix': uni((Lv, VMIX_RANK)),
        'w0': jnp.linspace(-6.0, -1.0, D_RWKV, dtype=f32)[None, :] + nrm((L, D_RWKV), 0.1),
        'w_decay_up': nrm((L, DECAY_RANK, D_RWKV), DECAY_RANK ** -0.5),
        'a0': nrm((L, D_RWKV), 0.1),
        'w_iclr_up': nrm((L, ICLR_RANK, D_RWKV), ICLR_RANK ** -0.5),
        'v0': nrm((Lv, D_RWKV), 0.1),
        'w_vmix_up': nrm((Lv, VMIX_RANK, D_RWKV), VMIX_RANK ** -0.5),
        'w_gate_up': nrm((L, GATE_RANK, D_RWKV), GATE_RANK ** -0.5),
        'k_k': 0.85 + nrm((L, D_RWKV), 0.05),
        'k_a': 1.0 + nrm((L, D_RWKV), 0.05),
        'r_k': -0.04 + nrm((L, N_HEADS, HEAD_DIM), 0.02),
        'gn_gain': 1.0 + nrm((L, D_RWKV), 0.05),
        'gn_bias': nrm((L, D_RWKV), 0.02),
        'dw_w': nrm((L, CONV_WIDTH, D_CONV), CONV_WIDTH ** -0.5),
        'dw_b': nrm((L, D_CONV), 0.02),
        'conv_ln_gain': 1.0 + nrm((L, D_CONV), 0.05),
        'conv_ln_bias': nrm((L, D_CONV), 0.02),
        'w_out': nrm((L, D_MIX, D_MODEL), D_MIX ** -0.5),
        'w_up': nrm((L, D_MODEL, D_FF), D_MODEL ** -0.5),
        'w_down': nrm((L, D_FF, D_MODEL), D_FF ** -0.5),
        'w_ple': nrm((L, D_PLE, D_MODEL), D_PLE ** -0.5),
        'norm_ple': 1.0 + nrm((L, D_MODEL), 0.05),
        'w_ple_gate': nrm((L, D_MODEL, D_MODEL), D_MODEL ** -0.5),
    }


def reference(x, p, norm_mix_pre, norm_mix_post, norm_mlp_pre, norm_mlp_post,
              w_in, w_in_vmix, mu_shift, mu_shift_vmix, w0, w_decay_up, a0,
              w_iclr_up, v0, w_vmix_up, w_gate_up, k_k, k_a, r_k, gn_gain, gn_bias,
              dw_w, dw_b, conv_ln_gain, conv_ln_bias, w_out, w_up, w_down,
              w_ple, norm_ple, w_ple_gate):
    v_first = None
    for i in range(DEPTH):
        h = rms_norm(x, norm_mix_pre[i])
        if i == 0:
            w_cat = w_in[0]
        else:
            w_cat = jnp.concatenate([w_in[i], w_in_vmix[i - 1]], axis=1)
        z = h @ w_cat
        z_rwkv = lerp_token_shift(z[..., :RWKV_COLS], mu_shift[i])
        z_conv = z[..., RWKV_COLS:D_IN]
        if i == 0:
            vmix = None
        else:
            vmix = (lerp_token_shift(z[..., D_IN:], mu_shift_vmix[i - 1]),
                    v0[i - 1], w_vmix_up[i - 1])
        y_rwkv, v_first = rwkv7_time_mix(
            z_rwkv, vmix, v_first, w0[i], w_decay_up[i], a0[i], w_iclr_up[i],
            w_gate_up[i], k_k[i], k_a[i], r_k[i], gn_gain[i], gn_bias[i])
        y_conv = conformer_conv(z_conv, dw_w[i], dw_b[i], conv_ln_gain[i], conv_ln_bias[i])
        mixed = jnp.concatenate([y_rwkv.astype(x.dtype), y_conv], axis=-1) @ w_out[i]
        x = x + rms_norm(mixed, norm_mix_post[i])
        h = rms_norm(x, norm_mlp_pre[i])
        f = jnp.square(jax.nn.relu(h @ w_up[i])) @ w_down[i]
        x = x + rms_norm(f, norm_mlp_post[i])
        e = rms_norm(p[i] @ w_ple[i], norm_ple[i])
        x = x + jax.nn.sigmoid(x @ w_ple_gate[i]) * e
    return x
```

```python
import functools

import jax
import jax.numpy as jnp
from jax import lax
from jax.experimental import pallas as pl
from jax.experimental.pallas import tpu as pltpu

F32 = jnp.float32
BF16 = jnp.bfloat16

HEAD_DIM = 64
DECAY_RANK = 64
ICLR_RANK = 64
VMIX_RANK = 32
GATE_RANK = 160
CONV_WIDTH = 31
RMS_EPS = 1e-6
GN_EPS = 64e-5
LN_EPS = 1e-5

LANES = 128
QUAD = 4
QW = QUAD * HEAD_DIM
CHUNK = 64
SM_DECAY, SM_ICLR, SM_GATE, SM_VMIX, SM_WIDTH = 0, 128, 256, 512, 640
CONV_HALO = 32
VMEM_LIMIT = 56 * 1024 * 1024

HIGHEST = lax.Precision.HIGHEST


def _cparams(*sem):
    return pltpu.CompilerParams(dimension_semantics=sem, vmem_limit_bytes=VMEM_LIMIT)


def _rms(x, g):
    return x * lax.rsqrt(jnp.mean(x * x, axis=-1, keepdims=True) + RMS_EPS) * g


def _split_dot(x, w_bf16):
    hi = x.astype(BF16)
    lo = (x - hi.astype(F32)).astype(BF16)
    n = x.shape[0]
    both = jnp.dot(jnp.concatenate([hi, lo], axis=0), w_bf16, preferred_element_type=F32)
    return both[:n] + both[n:]


def _inproj_kernel(x_ref, g_ref, w_ref, ws_ref, z_ref, zs_ref, h_ref):
    @pl.when(pl.program_id(1) == 0)
    def _():
        h_ref[...] = _rms(x_ref[...], g_ref[...]).astype(BF16)
        zs_ref[...] = jnp.dot(h_ref[...], ws_ref[...], preferred_element_type=F32)

    z_ref[...] = jnp.dot(h_ref[...], w_ref[...], preferred_element_type=F32)


def _inproj(x2, g, w_main, w_small, tm, tn):
    m, d = x2.shape
    n = w_main.shape[1]
    return pl.pallas_call(
        _inproj_kernel,
        grid=(m // tm, n // tn),
        in_specs=[
            pl.BlockSpec((tm, d), lambda i, j: (i, 0)),
            pl.BlockSpec((1, d), lambda i, j: (0, 0)),
            pl.BlockSpec((d, tn), lambda i, j: (0, j)),
            pl.BlockSpec((d, SM_WIDTH), lambda i, j: (0, 0)),
        ],
        out_specs=[
            pl.BlockSpec((tm, tn), lambda i, j: (i, j)),
            pl.BlockSpec((tm, SM_WIDTH), lambda i, j: (i, 0)),
        ],
        out_shape=[jax.ShapeDtypeStruct((m, n), F32), jax.ShapeDtypeStruct((m, SM_WIDTH), F32)],
        scratch_shapes=[pltpu.VMEM((tm, d), BF16)],
        compiler_params=_cparams("parallel", "arbitrary"),
        name="inproj",
    )(x2, g, w_main, w_small)


def _shift(z, prev_row, mu):
    rows = lax.broadcasted_iota(jnp.int32, z.shape, 0)
    prev = jnp.where(rows == 0, prev_row, pltpu.roll(z, 1, 0))
    return z + (prev - z) * mu


def _prep_kernel(has_vmix, *refs):
    (zr_ref, zk_ref, zv_ref, zs_ref, pr_ref, pk_ref, pv_ref, ps_ref, mu_ref, mus_ref,
     w0_ref, wdu_ref, a0_ref, wiu_ref, wgu_ref, kk_ref, ka_ref, e_ref) = refs[:18]
    refs = refs[18:]
    if has_vmix:
        vf_ref, v0_ref, wvu_ref = refs[:3]
        refs = refs[3:]
    r_out, lw_out, k_out, v_out, kk_out, b_out, g_out = refs

    first = pl.program_id(1) == 0

    def prev_row(p_ref):
        return jnp.where(first, 0.0, p_ref[0, 7:8, :])

    mu = mu_ref[...]
    d = zr_ref.shape[2]
    r = _shift(zr_ref[0], prev_row(pr_ref), mu[:, :d])
    k = _shift(zk_ref[0], prev_row(pk_ref), mu[:, d:2 * d])
    v = _shift(zv_ref[0], prev_row(pv_ref), mu[:, 2 * d:])
    zs = _shift(zs_ref[0], prev_row(ps_ref), mus_ref[...])

    def up(act, w_ref):
        return jnp.dot(act, w_ref[...], preferred_element_type=F32, precision=HIGHEST)

    w_raw = w0_ref[...] + up(jnp.tanh(zs[:, SM_DECAY:SM_ICLR]), wdu_ref)
    lw_out[0] = -jnp.exp(-0.5) * jax.nn.sigmoid(w_raw)
    iclr = jax.nn.sigmoid(a0_ref[...] + up(zs[:, SM_ICLR:SM_GATE], wiu_ref))
    g_out[0] = up(jax.nn.sigmoid(zs[:, SM_GATE:SM_VMIX]), wgu_ref)
    if has_vmix:
        mix = jax.nn.sigmoid(v0_ref[...] + up(zs[:, SM_VMIX:SM_WIDTH], wvu_ref))
        v = v + (vf_ref[0] - v) * mix
    v_out[0] = v
    r_out[0] = r

    kk = k * kk_ref[...]
    sq = kk * kk
    e = e_ref[...]
    n2 = jnp.concatenate(
        [_split_dot(sq[:, q * QW:(q + 1) * QW], e) for q in range(d // QW)], axis=1)
    kk = kk / jnp.maximum(jnp.sqrt(n2), 1e-12)
    kk_out[0] = kk
    b_out[0] = kk * iclr
    k_out[0] = k * (1.0 + (iclr - 1.0) * ka_ref[...])


def _prep(z_main, z_small, col0, mu, mus, w0, wdu, a0, wiu, wgu, k_k, k_a, e_ones, vmix, tp):
    b, s, _ = z_main.shape
    d = w0.shape[1]
    has_vmix = vmix is not None
    row = lambda bi, t: (bi, t, 0)
    prevrow = lambda bi, t: (bi, jnp.maximum(t * (tp // 8) - 1, 0), 0)

    def zcol(c):
        return pl.BlockSpec((1, tp, d), lambda bi, t: (bi, t, c))

    def pcol(c):
        return pl.BlockSpec((1, 8, d), lambda bi, t: (bi, jnp.maximum(t * (tp // 8) - 1, 0), c))

    def const(shape):
        return pl.BlockSpec(shape, lambda bi, t: (0,) * len(shape))

    in_specs = [zcol(col0), zcol(col0 + 1), zcol(col0 + 2), pl.BlockSpec((1, tp, SM_WIDTH), row),
                pcol(col0), pcol(col0 + 1), pcol(col0 + 2), pl.BlockSpec((1, 8, SM_WIDTH), prevrow),
                const((1, 3 * d)), const((1, SM_WIDTH)),
                const((1, d)), const((LANES, d)), const((1, d)), const((LANES, d)),
                const((SM_VMIX - SM_GATE, d)), const((1, d)), const((1, d)), const((QW, QW))]
    args = [z_main, z_main, z_main, z_small, z_main, z_main, z_main, z_small, mu, mus,
            w0, wdu, a0, wiu, wgu, k_k, k_a, e_ones]
    if has_vmix:
        v_first, v0, wvu = vmix
        in_specs += [pl.BlockSpec((1, tp, d), row), const((1, d)), const((LANES, d))]
        args += [v_first, v0, wvu]
    out = jax.ShapeDtypeStruct((b, s, d), F32)
    return pl.pallas_call(
        functools.partial(_prep_kernel, has_vmix),
        grid=(b, s // tp),
        in_specs=in_specs,
        out_specs=[pl.BlockSpec((1, tp, d), row)] * 7,
        out_shape=[out] * 7,
        compiler_params=_cparams("parallel", "arbitrary"),
        name="rwkv_prep",
    )(*args)


def _stack(x):
    head = lax.broadcasted_iota(jnp.int32, x.shape, 1) // HEAD_DIM
    return jnp.concatenate([jnp.where(head == h, x, 0.0) for h in range(QUAD)], axis=0).astype(BF16)


def _nt(a, b):
    return lax.dot_general(a, b, (((1,), (1,)), ((), ())), preferred_element_type=F32)


def _tn(a, b):
    return lax.dot_general(a, b, (((0,), (0,)), ((), ())), preferred_element_type=F32)


def _wkv_kernel(r_ref, lw_ref, k_ref, v_ref, kk_ref, b_ref, g_ref, rk_ref, gg_ref, gb_ref, e_ref,
                y_ref, s_ref):
    @pl.when(pl.program_id(2) == 0)
    def _():
        s_ref[...] = jnp.zeros_like(s_ref)

    c = r_ref.shape[1]
    n = QUAD * c
    r, lw, k, v = r_ref[0], lw_ref[0], k_ref[0], v_ref[0]

    ti = lax.broadcasted_iota(jnp.int32, (c, c), 0)
    tj = lax.broadcasted_iota(jnp.int32, (c, c), 1)
    cum = jnp.dot((ti >= tj).astype(F32), lw, preferred_element_type=F32, precision=HIGHEST)
    g_inv = jnp.exp(-cum)
    at = _stack(-kk_ref[0] * jnp.exp(cum - lw))
    rt = _stack(r * jnp.exp(cum))
    bt = _stack(b_ref[0] * g_inv)
    kt = _stack(k * g_inv)
    vs = _stack(v)

    gram = _nt(jnp.concatenate([at, rt], axis=0), jnp.concatenate([bt, kt], axis=0))
    si = lax.broadcasted_iota(jnp.int32, (n, n), 0)
    sj = lax.broadcasted_iota(jnp.int32, (n, n), 1)
    strict, incl = si > sj, si >= sj
    l_ab = jnp.where(strict, gram[:n, :n], 0.0)
    l_ak = jnp.where(strict, gram[:n, n:], 0.0).astype(BF16)
    p_rb = jnp.where(incl, gram[n:, :n], 0.0).astype(BF16)
    p_rk = jnp.where(incl, gram[n:, n:], 0.0).astype(BF16)

    t_inv = jnp.where(si == sj, 1.0, l_ab)
    power = l_ab.astype(BF16)
    steps = max(c.bit_length() - 2, 0)
    for _ in range(steps):
        power_f = jnp.dot(power, power, preferred_element_type=F32)
        power = power_f.astype(BF16)
        t_inv = t_inv + jnp.dot(power, t_inv.astype(BF16), preferred_element_type=F32)
    t_inv = t_inv.astype(BF16)

    state = s_ref[...]
    s_bf = state.astype(BF16)
    x = _nt(at, s_bf) + jnp.dot(l_ak, vs, preferred_element_type=F32)
    u = jnp.dot(t_inv, x.astype(BF16), preferred_element_type=F32).astype(BF16)
    ys = (_nt(rt, s_bf) + jnp.dot(p_rb, u, preferred_element_type=F32)
          + jnp.dot(p_rk, vs, preferred_element_type=F32))
    g_last = jnp.exp(cum[c - 1:c, :])
    s_ref[...] = (state + _tn(jnp.concatenate([u, vs], axis=0),
                              jnp.concatenate([bt, kt], axis=0))) * g_last

    y = ys[0:c]
    for h in range(1, QUAD):
        y = y + ys[h * c:(h + 1) * c]

    e = e_ref[...]
    inv_n = 1.0 / HEAD_DIM
    mean = _split_dot(y, e) * inv_n
    yc = y - mean
    var = _split_dot(yc * yc, e) * inv_n
    bonus = _split_dot(r * k * rk_ref[...], e) * v
    out = (yc * lax.rsqrt(var + GN_EPS) * gg_ref[...] + gb_ref[...] + bonus) * g_ref[0]
    y_ref[0] = out.astype(y_ref.dtype)


def _wkv(r, lw, k, v, kk, bb, gate, r_k, gn_gain, gn_bias, e_ones):
    b, s, d = r.shape
    blk = pl.BlockSpec((1, CHUNK, QW), lambda bi, q, c: (bi, c, q))
    vec = pl.BlockSpec((1, QW), lambda bi, q, c: (0, q))
    return pl.pallas_call(
        _wkv_kernel,
        grid=(b, d // QW, s // CHUNK),
        in_specs=[blk] * 7 + [vec] * 3 + [pl.BlockSpec((QW, QW), lambda bi, q, c: (0, 0))],
        out_specs=blk,
        out_shape=jax.ShapeDtypeStruct((b, s, d), BF16),
        scratch_shapes=[pltpu.VMEM((QW, QW), F32)],
        compiler_params=_cparams("parallel", "parallel", "arbitrary"),
        name="wkv",
    )(r, lw, k, v, kk, bb, gate, r_k, gn_gain, gn_bias, e_ones)


def _conv_kernel(z_ref, w_ref, b_ref, lg_ref, lb_ref, y_ref, u_ref):
    tc = z_ref.shape[1]
    d = y_ref.shape[2]

    @pl.when(pl.program_id(1) == 0)
    def _():
        u_ref[0:CONV_HALO, :] = jnp.zeros((CONV_HALO, d), F32)

    z = z_ref[0]
    u_ref[CONV_HALO:CONV_HALO + tc, :] = z[:, :d] * jax.nn.sigmoid(z[:, d:])

    base = CONV_HALO - (CONV_WIDTH - 1)
    acc = jnp.broadcast_to(b_ref[...], (tc, d))
    for j in range(CONV_WIDTH):
        acc = acc + u_ref[base + j:base + j + tc, :] * w_ref[j:j + 1, :]
    mean = jnp.mean(acc, axis=-1, keepdims=True)
    cen = acc - mean
    var = jnp.mean(cen * cen, axis=-1, keepdims=True)
    cf = cen * lax.rsqrt(var + LN_EPS) * lg_ref[...] + lb_ref[...]
    y_ref[0] = (cf * jax.nn.sigmoid(cf)).astype(y_ref.dtype)
    u_ref[0:CONV_HALO, :] = u_ref[tc:tc + CONV_HALO, :]


def _conv(z_main, dw_w, dw_b, ln_g, ln_b, tc):
    b, s, _ = z_main.shape
    d = dw_w.shape[1]
    const = lambda shape: pl.BlockSpec(shape, lambda bi, t: (0, 0))
    return pl.pallas_call(
        _conv_kernel,
        grid=(b, s // tc),
        in_specs=[pl.BlockSpec((1, tc, 2 * d), lambda bi, t: (bi, t, 0)),
                  const((CONV_WIDTH, d)), const((1, d)), const((1, d)), const((1, d))],
        out_specs=pl.BlockSpec((1, tc, d), lambda bi, t: (bi, t, 0)),
        out_shape=jax.ShapeDtypeStruct((b, s, d), BF16),
        scratch_shapes=[pltpu.VMEM((tc + CONV_HALO, d), F32)],
        compiler_params=_cparams("parallel", "arbitrary"),
        name="conformer_conv",
    )(z_main, dw_w, dw_b, ln_g, ln_b)


def _outproj_kernel(yr_ref, yc_ref, x_ref, wr_ref, wc_ref, g_ref, o_ref):
    mixed = (jnp.dot(yr_ref[...], wr_ref[...], preferred_element_type=F32)
             + jnp.dot(yc_ref[...], wc_ref[...], preferred_element_type=F32))
    o_ref[...] = x_ref[...] + _rms(mixed, g_ref[...])


def _outproj(y_r, y_c, x2, w_r, w_c, g, tm):
    m, d = x2.shape
    dr, dc = y_r.shape[1], y_c.shape[1]
    return pl.pallas_call(
        _outproj_kernel,
        grid=(m // tm,),
        in_specs=[pl.BlockSpec((tm, dr), lambda i: (i, 0)),
                  pl.BlockSpec((tm, dc), lambda i: (i, 0)),
                  pl.BlockSpec((tm, d), lambda i: (i, 0)),
                  pl.BlockSpec((dr, d), lambda i: (0, 0)),
                  pl.BlockSpec((dc, d), lambda i: (0, 0)),
                  pl.BlockSpec((1, d), lambda i: (0, 0))],
        out_specs=pl.BlockSpec((tm, d), lambda i: (i, 0)),
        out_shape=jax.ShapeDtypeStruct((m, d), F32),
        compiler_params=_cparams("parallel"),
        name="outproj",
    )(y_r, y_c, x2, w_r, w_c, g)


def _mlp_kernel(x_ref, g1_ref, wu_ref, wd_ref, g2_ref, o_ref, h_ref, acc_ref):
    j = pl.program_id(1)

    @pl.when(j == 0)
    def _():
        h_ref[...] = _rms(x_ref[...], g1_ref[...]).astype(BF16)
        acc_ref[...] = jnp.zeros_like(acc_ref)

    a = jnp.maximum(jnp.dot(h_ref[...], wu_ref[...], preferred_element_type=F32), 0.0)
    acc_ref[...] += jnp.dot((a * a).astype(BF16), wd_ref[...], preferred_element_type=F32)

    @pl.when(j == pl.num_programs(1) - 1)
    def _():
        o_ref[...] = x_ref[...] + _rms(acc_ref[...], g2_ref[...])


def _mlp(x2, g1, w_up, w_down, g2, tm, tf):
    m, d = x2.shape
    f = w_up.shape[1]
    return pl.pallas_call(
        _mlp_kernel,
        grid=(m // tm, f // tf),
        in_specs=[pl.BlockSpec((tm, d), lambda i, j: (i, 0)),
                  pl.BlockSpec((1, d), lambda i, j: (0, 0)),
                  pl.BlockSpec((d, tf), lambda i, j: (0, j)),
                  pl.BlockSpec((tf, d), lambda i, j: (j, 0)),
                  pl.BlockSpec((1, d), lambda i, j: (0, 0))],
        out_specs=pl.BlockSpec((tm, d), lambda i, j: (i, 0)),
        out_shape=jax.ShapeDtypeStruct((m, d), F32),
        scratch_shapes=[pltpu.VMEM((tm, d), BF16), pltpu.VMEM((tm, d), F32)],
        compiler_params=_cparams("parallel", "arbitrary"),
        name="mlp",
    )(x2, g1, w_up, w_down, g2)


def _ple_kernel(x_ref, p_ref, wp_ref, g_ref, wg_ref, o_ref):
    x = x_ref[...]
    e = _rms(jnp.dot(p_ref[...].astype(BF16), wp_ref[...], preferred_element_type=F32), g_ref[...])
    gate = jax.nn.sigmoid(jnp.dot(x.astype(BF16), wg_ref[...], preferred_element_type=F32))
    o_ref[...] = x + gate * e


def _ple(x2, p2, w_ple, g, w_gate, tm):
    m, d = x2.shape
    dp = p2.shape[1]
    return pl.pallas_call(
        _ple_kernel,
        grid=(m // tm,),
        in_specs=[pl.BlockSpec((tm, d), lambda i: (i, 0)),
                  pl.BlockSpec((tm, dp), lambda i: (i, 0)),
                  pl.BlockSpec((dp, d), lambda i: (0, 0)),
                  pl.BlockSpec((1, d), lambda i: (0, 0)),
                  pl.BlockSpec((d, d), lambda i: (0, 0))],
        out_specs=pl.BlockSpec((tm, d), lambda i: (i, 0)),
        out_shape=jax.ShapeDtypeStruct((m, d), F32),
        compiler_params=_cparams("parallel"),
        name="ple",
    )(x2, p2, w_ple, g, w_gate)


def _tile(n, want, align):
    t = (min(n, want) // align) * align
    while n % t:
        t -= align
    return t


def _pad_rows(w, rows):
    return jnp.pad(w, ((0, rows - w.shape[0]), (0, 0)))


def kernel(x, p, norm_mix_pre, norm_mix_post, norm_mlp_pre, norm_mlp_post, w_in, w_in_vmix, mu_shift, mu_shift_vmix, w0, w_decay_up, a0, w_iclr_up, v0, w_vmix_up, w_gate_up, k_k, k_a, r_k, gn_gain, gn_bias, dw_w, dw_b, conv_ln_gain, conv_ln_bias, w_out, w_up, w_down, w_ple, norm_ple, w_ple_gate):
    bsz, seq, d_model = x.shape
    depth = w_in.shape[0]
    d_rwkv = w0.shape[1]
    d_conv = dw_w.shape[2]
    m = bsz * seq
    rkv = 3 * d_rwkv
    low = DECAY_RANK + ICLR_RANK + GATE_RANK
    assert d_rwkv % QW == 0 and seq % CHUNK == 0 and (2 * d_conv) % d_rwkv == 0

    tm = _tile(m, 512, 8)
    tn = _tile(2 * d_conv + rkv, 1024, LANES)
    tf = _tile(w_up.shape[2], 1024, LANES)
    tp = _tile(seq, 256, CONV_HALO)
    tc = _tile(seq, 256, CONV_HALO)
    col0 = (2 * d_conv) // d_rwkv

    head = jnp.arange(QW) // HEAD_DIM
    e_ones = (head[:, None] == head[None, :]).astype(BF16)
    row = lambda a: a.reshape(1, -1)

    def small_cols(cols, width):
        return jnp.pad(cols, ((0, 0), (0, width - cols.shape[1])))

    x2 = x.reshape(m, d_model)
    v_first = None
    for i in range(depth):
        wi = w_in[i]
        w_main = jnp.concatenate([wi[:, rkv + low:], wi[:, :rkv]], axis=1).astype(BF16)
        d0, d1, d2 = rkv, rkv + DECAY_RANK, rkv + DECAY_RANK + ICLR_RANK
        ws = [small_cols(wi[:, d0:d1], SM_ICLR - SM_DECAY),
              small_cols(wi[:, d1:d2], SM_GATE - SM_ICLR),
              small_cols(wi[:, d2:rkv + low], SM_VMIX - SM_GATE)]
        ms = [small_cols(row(mu_shift[i, d0:d1]), SM_ICLR - SM_DECAY),
              small_cols(row(mu_shift[i, d1:d2]), SM_GATE - SM_ICLR),
              small_cols(row(mu_shift[i, d2:rkv + low]), SM_VMIX - SM_GATE)]
        if i == 0:
            ws.append(jnp.zeros((d_model, SM_WIDTH - SM_VMIX), F32))
            ms.append(jnp.zeros((1, SM_WIDTH - SM_VMIX), F32))
            vmix = None
        else:
            ws.append(small_cols(w_in_vmix[i - 1], SM_WIDTH - SM_VMIX))
            ms.append(small_cols(row(mu_shift_vmix[i - 1]), SM_WIDTH - SM_VMIX))
            vmix = (v_first, row(v0[i - 1]), _pad_rows(w_vmix_up[i - 1], SM_WIDTH - SM_VMIX))
        w_small = jnp.concatenate(ws, axis=1).astype(BF16)
        mu_small = jnp.concatenate(ms, axis=1)

        z_main, z_small = _inproj(x2, row(norm_mix_pre[i]), w_main, w_small, tm, tn)
        z_main = z_main.reshape(bsz, seq, -1)
        z_small = z_small.reshape(bsz, seq, SM_WIDTH)

        r, lw, k, v, kk, bb, gate = _prep(
            z_main, z_small, col0, row(mu_shift[i, :rkv]), mu_small,
            row(w0[i]), _pad_rows(w_decay_up[i], SM_ICLR - SM_DECAY),
            row(a0[i]), _pad_rows(w_iclr_up[i], SM_GATE - SM_ICLR),
            _pad_rows(w_gate_up[i], SM_VMIX - SM_GATE),
            row(k_k[i]), row(k_a[i]), e_ones, vmix, tp)
        if i == 0:
            v_first = v
        y_r = _wkv(r, lw, k, v, kk, bb, gate, row(r_k[i]), row(gn_gain[i]), row(gn_bias[i]), e_ones)
        y_c = _conv(z_main, dw_w[i], row(dw_b[i]), row(conv_ln_gain[i]), row(conv_ln_bias[i]), tc)

        wo = w_out[i].astype(BF16)
        x2 = _outproj(y_r.reshape(m, d_rwkv), y_c.reshape(m, d_conv), x2,
                      wo[:d_rwkv], wo[d_rwkv:], row(norm_mix_post[i]), tm)
        x2 = _mlp(x2, row(norm_mlp_pre[i]), w_up[i].astype(BF16), w_down[i].astype(BF16),
                  row(norm_mlp_post[i]), tm, tf)
        x2 = _ple(x2, p[i].reshape(m, -1), w_ple[i].astype(BF16), row(norm_ple[i]),
                  w_ple_gate[i].astype(BF16), tm)
    return x2.reshape(bsz, seq, d_model)
```

```python
import functools

import jax
import jax.numpy as jnp
from jax import lax
from jax.experimental import pallas as pl
from jax.experimental.pallas import tpu as pltpu

F32 = jnp.float32
BF16 = jnp.bfloat16

HEAD_DIM = 64
DECAY_RANK = 64
ICLR_RANK = 64
VMIX_RANK = 32
GATE_RANK = 160
CONV_WIDTH = 31
RMS_EPS = 1e-6
GN_EPS = 64e-5
LN_EPS = 1e-5

LANES = 128
SUBLANES = 8
CONV_ROWS = 128
QUAD = 4
QW = QUAD * HEAD_DIM
CHUNK = 64
WKV_ROWS = 2
SM_DECAY, SM_ICLR, SM_GATE, SM_VMIX, SM_WIDTH = 0, 128, 256, 512, 640
CONV_HALO = 32
VMEM_LIMIT = 56 * 1024 * 1024


def _cparams(*sem):
    return pltpu.CompilerParams(dimension_semantics=sem, vmem_limit_bytes=VMEM_LIMIT)


def _rms(x, g):
    return x * lax.rsqrt(jnp.mean(x * x, axis=-1, keepdims=True) + RMS_EPS) * g


def _split_dot(x, w_bf16):
    hi = x.astype(BF16)
    lo = (x - hi.astype(F32)).astype(BF16)
    n = x.shape[0]
    both = jnp.dot(jnp.concatenate([hi, lo], axis=0), w_bf16, preferred_element_type=F32)
    return both[:n] + both[n:]


def _inproj_kernel(x_ref, g_ref, w_ref, ws_ref, z_ref, zs_ref, h_ref):
    @pl.when(pl.program_id(1) == 0)
    def _():
        h_ref[...] = _rms(x_ref[...], g_ref[...]).astype(BF16)
        zs_ref[...] = jnp.dot(h_ref[...], ws_ref[...], preferred_element_type=F32)

    z_ref[...] = jnp.dot(h_ref[...], w_ref[...], preferred_element_type=F32)


def _inproj(x2, g, w_main, w_small, tm, tn):
    m, d = x2.shape
    n = w_main.shape[1]
    return pl.pallas_call(
        _inproj_kernel,
        grid=(m // tm, n // tn),
        in_specs=[
            pl.BlockSpec((tm, d), lambda i, j: (i, 0)),
            pl.BlockSpec((1, d), lambda i, j: (0, 0)),
            pl.BlockSpec((d, tn), lambda i, j: (0, j)),
            pl.BlockSpec((d, SM_WIDTH), lambda i, j: (0, 0)),
        ],
        out_specs=[
            pl.BlockSpec((tm, tn), lambda i, j: (i, j)),
            pl.BlockSpec((tm, SM_WIDTH), lambda i, j: (i, 0)),
        ],
        out_shape=[jax.ShapeDtypeStruct((m, n), F32), jax.ShapeDtypeStruct((m, SM_WIDTH), F32)],
        scratch_shapes=[pltpu.VMEM((tm, d), BF16)],
        compiler_params=_cparams("parallel", "arbitrary"),
        name="inproj",
    )(x2, g, w_main, w_small)


def _shift(z, prev_row, mu):
    rows = lax.broadcasted_iota(jnp.int32, z.shape, 0)
    prev = jnp.where(rows == 0, prev_row, pltpu.roll(z, 1, 0))
    return z + (prev - z) * mu


def _prep_kernel(has_vmix, *refs):
    (zr_ref, zk_ref, zv_ref, zs_ref, pr_ref, pk_ref, pv_ref, ps_ref, mu_ref, mus_ref,
     w0_ref, wdu_ref, a0_ref, wiu_ref, wgu_ref, kk_ref, ka_ref, rk_ref, e_ref) = refs[:19]
    refs = refs[19:]
    if has_vmix:
        vf_ref, v0_ref, wvu_ref = refs[:3]
        refs = refs[3:]
    at_out, rt_out, bt_out, kt_out, v_out, gl_out, g_out, bo_out = refs[:8]

    first = pl.program_id(1) == 0

    def prev_row(p_ref):
        return jnp.where(first, 0.0, p_ref[0, 7:8, :])

    mu = mu_ref[...]
    d = zr_ref.shape[2]
    r = _shift(zr_ref[0], prev_row(pr_ref), mu[:, :d])
    k = _shift(zk_ref[0], prev_row(pk_ref), mu[:, d:2 * d])
    v = _shift(zv_ref[0], prev_row(pv_ref), mu[:, 2 * d:])
    zs = _shift(zs_ref[0], prev_row(ps_ref), mus_ref[...])

    def up(act, w_ref):
        hi = act.astype(BF16)
        lo = (act - hi.astype(F32)).astype(BF16)
        n = act.shape[0]
        both = jnp.dot(jnp.concatenate([hi, lo], axis=0), w_ref[0], preferred_element_type=F32)
        return both[:n] + both[n:] + jnp.dot(hi, w_ref[1], preferred_element_type=F32)

    w_raw = w0_ref[...] + up(jnp.tanh(zs[:, SM_DECAY:SM_ICLR]), wdu_ref)
    lw = -jnp.exp(-0.5) * jax.nn.sigmoid(w_raw)
    iclr = jax.nn.sigmoid(a0_ref[...] + up(zs[:, SM_ICLR:SM_GATE], wiu_ref))
    g_out[0] = up(jax.nn.sigmoid(zs[:, SM_GATE:SM_VMIX]), wgu_ref)
    if has_vmix:
        mix = jax.nn.sigmoid(v0_ref[...] + up(zs[:, SM_VMIX:SM_WIDTH], wvu_ref))
        v = v + (vf_ref[0] - v) * mix
    else:
        refs[8][0] = v
    v_out[0] = v.astype(BF16)

    e = e_ref[...]

    def head_sum(t):
        return jnp.concatenate(
            [_split_dot(t[:, q * QW:(q + 1) * QW], e) for q in range(d // QW)], axis=1)

    kk = k * kk_ref[...]
    kk = kk / jnp.maximum(jnp.sqrt(head_sum(kk * kk)), 1e-12)
    k = k * (1.0 + (iclr - 1.0) * ka_ref[...])
    bo_out[0] = head_sum(r * k * rk_ref[...]) * v

    tp = lw.shape[0]
    ti = lax.broadcasted_iota(jnp.int32, (tp, tp), 0)
    tj = lax.broadcasted_iota(jnp.int32, (tp, tp), 1)
    tri = ((ti >= tj) & (ti // CHUNK == tj // CHUNK)).astype(BF16)
    hi = lw.astype(BF16)
    rest = lw - hi.astype(F32)
    mid = rest.astype(BF16)
    lo = (rest - mid.astype(F32)).astype(BF16)
    cum = (jnp.dot(tri, hi, preferred_element_type=F32) + jnp.dot(tri, mid, preferred_element_type=F32)
           + jnp.dot(tri, lo, preferred_element_type=F32))
    g_inv = jnp.exp(-cum)
    at_out[0] = (-kk * jnp.exp(cum - lw)).astype(BF16)
    rt_out[0] = (r * jnp.exp(cum)).astype(BF16)
    bt_out[0] = (kk * iclr * g_inv).astype(BF16)
    kt_out[0] = (k * g_inv).astype(BF16)
    for j in range(tp // CHUNK):
        last = (j + 1) * CHUNK - 1
        gl_out[0, j] = jnp.exp(cum[last:last + 1, :])


def _prep(z_main, z_small, col0, mu, mus, w0, wdu, a0, wiu, wgu, k_k, k_a, r_k, e_ones, vmix, tp):
    b, s, _ = z_main.shape
    d = w0.shape[1]
    has_vmix = vmix is not None
    row = lambda bi, t: (bi, t, 0)
    prevrow = lambda bi, t: (bi, jnp.maximum(t * (tp // 8) - 1, 0), 0)

    def zcol(c):
        return pl.BlockSpec((1, tp, d), lambda bi, t: (bi, t, c))

    def pcol(c):
        return pl.BlockSpec((1, 8, d), lambda bi, t: (bi, jnp.maximum(t * (tp // 8) - 1, 0), c))

    def const(shape):
        return pl.BlockSpec(shape, lambda bi, t: (0,) * len(shape))

    in_specs = [zcol(col0), zcol(col0 + 1), zcol(col0 + 2), pl.BlockSpec((1, tp, SM_WIDTH), row),
                pcol(col0), pcol(col0 + 1), pcol(col0 + 2), pl.BlockSpec((1, 8, SM_WIDTH), prevrow),
                const((1, 3 * d)), const((1, SM_WIDTH)),
                const((1, d)), const((2, LANES, d)), const((1, d)), const((2, LANES, d)),
                const((2, SM_VMIX - SM_GATE, d)), const((1, d)), const((1, d)), const((1, d)),
                const((QW, QW))]
    args = [z_main, z_main, z_main, z_small, z_main, z_main, z_main, z_small, mu, mus,
            w0, wdu, a0, wiu, wgu, k_k, k_a, r_k, e_ones]
    if has_vmix:
        v_first, v0, wvu = vmix
        in_specs += [pl.BlockSpec((1, tp, d), row), const((1, d)), const((2, LANES, d))]
        args += [v_first, v0, wvu]
    tile = pl.BlockSpec((1, tp, d), row)
    nch = tp // CHUNK
    out_specs = [tile] * 5 + [pl.BlockSpec((1, nch, 1, d), lambda bi, t: (bi, t, 0, 0)), tile, tile]
    out_shape = ([jax.ShapeDtypeStruct((b, s, d), BF16)] * 5
                 + [jax.ShapeDtypeStruct((b, s // CHUNK, 1, d), F32)]
                 + [jax.ShapeDtypeStruct((b, s, d), F32)] * 2)
    if not has_vmix:
        out_specs.append(tile)
        out_shape.append(jax.ShapeDtypeStruct((b, s, d), F32))
    return pl.pallas_call(
        functools.partial(_prep_kernel, has_vmix),
        grid=(b, s // tp),
        in_specs=in_specs,
        out_specs=out_specs,
        out_shape=out_shape,
        compiler_params=_cparams("parallel", "arbitrary"),
        name="rwkv_prep",
    )(*args)


def _stack(x, block_diag):
    return jnp.where(block_diag, jnp.concatenate([x] * QUAD, axis=0), jnp.zeros((), x.dtype))


def _nt(a, b):
    return lax.dot_general(a, b, (((1,), (1,)), ((), ())), preferred_element_type=F32)


def _tn(a, b):
    return lax.dot_general(a, b, (((0,), (0,)), ((), ())), preferred_element_type=F32)


def _mm(a, b):
    return jnp.dot(a, b, preferred_element_type=F32)


def _wkv_kernel(at_ref, rt_ref, bt_ref, kt_ref, v_ref, gl_ref, g_ref, bo_ref, gg_ref, gb_ref,
                e_ref, y_ref, s_ref):
    @pl.when(pl.program_id(1) == 0)
    def _():
        s_ref[...] = jnp.zeros_like(s_ref)

    nb, c, d = at_ref.shape
    nq = d // QW
    quads = range(nb * nq)
    ri = lax.broadcasted_iota(jnp.int32, (QUAD * c, QW), 0)
    li = lax.broadcasted_iota(jnp.int32, (QUAD * c, QW), 1)
    ti = lax.broadcasted_iota(jnp.int32, (c, QW), 0)
    si = lax.broadcasted_iota(jnp.int32, (c, QW), 1) % c
    block_diag, strict, incl, eye = ri // c == li // HEAD_DIM, ti > si, ti >= si, ti == si
    bf = lambda t: t.astype(BF16)
    stack = lambda t: _stack(t, block_diag)
    cat = lambda *t: jnp.concatenate(t, axis=0)
    lanes = lambda ref, q: ref[q // nq, :, (q % nq) * QW:(q % nq + 1) * QW]

    bt = [lanes(bt_ref, q) for q in quads]
    kt = [lanes(kt_ref, q) for q in quads]
    vv = [lanes(v_ref, q) for q in quads]
    lhs = [cat(lanes(at_ref, q), lanes(rt_ref, q)) for q in quads]
    gram = [_nt(lhs[q], cat(stack(bt[q]), stack(kt[q]))) for q in quads]
    l_ab = [jnp.where(strict, g[:c, :QW], 0.0) for g in gram]
    tri_k = [bf(cat(jnp.where(strict, g[:c, QW:], 0.0), jnp.where(incl, g[c:, QW:], 0.0)))
             for g in gram]
    p_rb = [bf(jnp.where(incl, g[c:, :QW], 0.0)) for g in gram]

    state = [s_ref[q] for q in quads]
    from_state = [_nt(lhs[q], bf(state[q])) for q in quads]
    from_v = [_mm(tri_k[q], stack(vv[q])) for q in quads]

    t_inv = [jnp.where(eye, 1.0, l) for l in l_ab]
    power = [bf(l) for l in l_ab]
    power = [bf(_mm(p, stack(p))) for p in power]
    for _ in range(c.bit_length() - 3):
        both = [_mm(cat(power[q], bf(t_inv[q])), stack(power[q])) for q in quads]
        power = [bf(b[:c]) for b in both]
        t_inv = [t_inv[q] + both[q][c:] for q in quads]
    t_inv = [t_inv[q] + _mm(bf(t_inv[q]), stack(power[q])) for q in quads]

    u = [bf(_mm(bf(t_inv[q]), stack(bf(from_state[q][:c] + from_v[q][:c])))) for q in quads]
    y = [from_state[q][c:] + from_v[q][c:] + _mm(p_rb[q], stack(u[q])) for q in quads]
    upd = [_tn(cat(u[q], vv[q]), cat(bt[q], kt[q])) for q in quads]
    for q in quads:
        g_last = gl_ref[q // nq, 0, :, (q % nq) * QW:(q % nq + 1) * QW]
        s_ref[q] = (state[q] + jnp.where(block_diag, upd[q], 0.0)) * g_last

    e = e_ref[...]
    inv_n = 1.0 / HEAD_DIM
    mean = [_split_dot(t, e) * inv_n for t in y]
    yc = [y[q] - mean[q] for q in quads]
    var = [_split_dot(t * t, e) * inv_n for t in yc]
    for q in quads:
        bi, sl = q // nq, slice((q % nq) * QW, (q % nq + 1) * QW)
        out = (yc[q] * lax.rsqrt(var[q] + GN_EPS) * gg_ref[:, sl] + gb_ref[:, sl]
               + bo_ref[bi, :, sl]) * g_ref[bi, :, sl]
        y_ref[bi, :, sl] = out.astype(y_ref.dtype)


def _wkv(at, rt, bt, kt, v, gl, gate, bonus, gn_gain, gn_bias, e_ones, nb):
    assert CHUNK == HEAD_DIM
    b, s, d = at.shape
    blk = pl.BlockSpec((nb, CHUNK, d), lambda bi, c: (bi, c, 0))
    vec = pl.BlockSpec((1, d), lambda bi, c: (0, 0))
    return pl.pallas_call(
        _wkv_kernel,
        grid=(b // nb, s // CHUNK),
        in_specs=([blk] * 5 + [pl.BlockSpec((nb, 1, 1, d), lambda bi, c: (bi, c, 0, 0))]
                  + [blk] * 2 + [vec] * 2 + [pl.BlockSpec((QW, QW), lambda bi, c: (0, 0))]),
        out_specs=blk,
        out_shape=jax.ShapeDtypeStruct((b, s, d), BF16),
        scratch_shapes=[pltpu.VMEM((nb * d // QW, QW, QW), F32)],
        compiler_params=_cparams("parallel", "arbitrary"),
        name="wkv",
    )(at, rt, bt, kt, v, gl, gate, bonus, gn_gain, gn_bias, e_ones)


def _conv_kernel(z_ref, w_ref, b_ref, lg_ref, lb_ref, y_ref, u_ref, c_ref, sh_ref):
    tc = z_ref.shape[1]
    d = y_ref.shape[2]

    @pl.when(pl.program_id(1) == 0)
    def _():
        u_ref[0:CONV_HALO, :] = jnp.zeros((CONV_HALO, d), F32)

    z = z_ref[0]
    u_ref[CONV_HALO:CONV_HALO + tc, :] = z[:, :d] * jax.nn.sigmoid(z[:, d:])

    base = CONV_HALO - (CONV_WIDTH - 1)
    rb = min(tc, CONV_ROWS)
    nt = rb // SUBLANES

    def strip(s, carry):
        cols = pl.ds(pl.multiple_of(s * LANES, LANES), LANES)
        bias = jnp.broadcast_to(b_ref[:, cols], (SUBLANES, LANES))
        for r0 in range(0, tc, rb):
            acc = [bias] * nt
            for res in range(SUBLANES):
                offs = [o for o in range(base, base + CONV_WIDTH) if o % SUBLANES == res]
                lo, hi = offs[0], offs[-1]
                n = hi - lo + rb
                if res:
                    sh_ref[res, 0:n, :] = u_ref[r0 + lo:r0 + lo + n, cols]
                    src, at = sh_ref.at[res], 0
                else:
                    src, at = u_ref.at[:, cols], r0 + lo
                win = [src[at + i * SUBLANES:at + (i + 1) * SUBLANES, :] for i in range(n // SUBLANES)]
                for o in offs:
                    tap = jnp.broadcast_to(w_ref[o - base:o - base + 1, cols], (SUBLANES, LANES))
                    k = (o - lo) // SUBLANES
                    acc = [acc[i] + win[i + k] * tap for i in range(nt)]
            for i in range(nt):
                c_ref[r0 + i * SUBLANES:r0 + (i + 1) * SUBLANES, cols] = acc[i]
        return carry

    lax.fori_loop(0, d // LANES, strip, 0)
    acc = c_ref[...]
    mean = jnp.mean(acc, axis=-1, keepdims=True)
    cen = acc - mean
    var = jnp.mean(cen * cen, axis=-1, keepdims=True)
    cf = cen * lax.rsqrt(var + LN_EPS) * lg_ref[...] + lb_ref[...]
    y_ref[0] = (cf * jax.nn.sigmoid(cf)).astype(y_ref.dtype)
    u_ref[0:CONV_HALO, :] = u_ref[tc:tc + CONV_HALO, :]


def _conv(z_main, dw_w, dw_b, ln_g, ln_b, tc):
    b, s, _ = z_main.shape
    d = dw_w.shape[1]
    const = lambda shape: pl.BlockSpec(shape, lambda bi, t: (0, 0))
    return pl.pallas_call(
        _conv_kernel,
        grid=(b, s // tc),
        in_specs=[pl.BlockSpec((1, tc, 2 * d), lambda bi, t: (bi, t, 0)),
                  const((CONV_WIDTH, d)), const((1, d)), const((1, d)), const((1, d))],
        out_specs=pl.BlockSpec((1, tc, d), lambda bi, t: (bi, t, 0)),
        out_shape=jax.ShapeDtypeStruct((b, s, d), BF16),
        scratch_shapes=[pltpu.VMEM((tc + CONV_HALO, d), F32), pltpu.VMEM((tc, d), F32),
                        pltpu.VMEM((SUBLANES, min(tc, CONV_ROWS) + CONV_HALO, LANES), F32)],
        compiler_params=_cparams("parallel", "arbitrary"),
        name="conformer_conv",
    )(z_main, dw_w, dw_b, ln_g, ln_b)


def _outproj_kernel(yr_ref, yc_ref, x_ref, wr_ref, wc_ref, g_ref, o_ref):
    mixed = (jnp.dot(yr_ref[...], wr_ref[...], preferred_element_type=F32)
             + jnp.dot(yc_ref[...], wc_ref[...], preferred_element_type=F32))
    o_ref[...] = x_ref[...] + _rms(mixed, g_ref[...])


def _outproj(y_r, y_c, x2, w_r, w_c, g, tm):
    m, d = x2.shape
    dr, dc = y_r.shape[1], y_c.shape[1]
    return pl.pallas_call(
        _outproj_kernel,
        grid=(m // tm,),
        in_specs=[pl.BlockSpec((tm, dr), lambda i: (i, 0)),
                  pl.BlockSpec((tm, dc), lambda i: (i, 0)),
                  pl.BlockSpec((tm, d), lambda i: (i, 0)),
                  pl.BlockSpec((dr, d), lambda i: (0, 0)),
                  pl.BlockSpec((dc, d), lambda i: (0, 0)),
                  pl.BlockSpec((1, d), lambda i: (0, 0))],
        out_specs=pl.BlockSpec((tm, d), lambda i: (i, 0)),
        out_shape=jax.ShapeDtypeStruct((m, d), F32),
        compiler_params=_cparams("parallel"),
        name="outproj",
    )(y_r, y_c, x2, w_r, w_c, g)


def _mlp_kernel(x_ref, g1_ref, wu_ref, wd_ref, g2_ref, o_ref, h_ref, acc_ref):
    j = pl.program_id(1)

    @pl.when(j == 0)
    def _():
        h_ref[...] = _rms(x_ref[...], g1_ref[...]).astype(BF16)
        acc_ref[...] = jnp.zeros_like(acc_ref)

    a = jnp.maximum(jnp.dot(h_ref[...], wu_ref[...], preferred_element_type=F32), 0.0)
    acc_ref[...] += jnp.dot((a * a).astype(BF16), wd_ref[...], preferred_element_type=F32)

    @pl.when(j == pl.num_programs(1) - 1)
    def _():
        o_ref[...] = x_ref[...] + _rms(acc_ref[...], g2_ref[...])


def _mlp(x2, g1, w_up, w_down, g2, tm, tf):
    m, d = x2.shape
    f = w_up.shape[1]
    return pl.pallas_call(
        _mlp_kernel,
        grid=(m // tm, f // tf),
        in_specs=[pl.BlockSpec((tm, d), lambda i, j: (i, 0)),
                  pl.BlockSpec((1, d), lambda i, j: (0, 0)),
                  pl.BlockSpec((d, tf), lambda i, j: (0, j)),
                  pl.BlockSpec((tf, d), lambda i, j: (j, 0)),
                  pl.BlockSpec((1, d), lambda i, j: (0, 0))],
        out_specs=pl.BlockSpec((tm, d), lambda i, j: (i, 0)),
        out_shape=jax.ShapeDtypeStruct((m, d), F32),
        scratch_shapes=[pltpu.VMEM((tm, d), BF16), pltpu.VMEM((tm, d), F32)],
        compiler_params=_cparams("parallel", "arbitrary"),
        name="mlp",
    )(x2, g1, w_up, w_down, g2)


def _ple_kernel(x_ref, p_ref, wp_ref, g_ref, wg_ref, o_ref):
    x = x_ref[...]
    e = _rms(jnp.dot(p_ref[...].astype(BF16), wp_ref[...], preferred_element_type=F32), g_ref[...])
    gate = jax.nn.sigmoid(jnp.dot(x.astype(BF16), wg_ref[...], preferred_element_type=F32))
    o_ref[...] = x + gate * e


def _ple(x2, p2, w_ple, g, w_gate, tm):
    m, d = x2.shape
    dp = p2.shape[1]
    return pl.pallas_call(
        _ple_kernel,
        grid=(m // tm,),
        in_specs=[pl.BlockSpec((tm, d), lambda i: (i, 0)),
                  pl.BlockSpec((tm, dp), lambda i: (i, 0)),
                  pl.BlockSpec((dp, d), lambda i: (0, 0)),
                  pl.BlockSpec((1, d), lambda i: (0, 0)),
                  pl.BlockSpec((d, d), lambda i: (0, 0))],
        out_specs=pl.BlockSpec((tm, d), lambda i: (i, 0)),
        out_shape=jax.ShapeDtypeStruct((m, d), F32),
        compiler_params=_cparams("parallel"),
        name="ple",
    )(x2, p2, w_ple, g, w_gate)


def _tile(n, want, align):
    t = (min(n, want) // align) * align
    while n % t:
        t -= align
    return t


def _up_weight(w, rows):
    w = jnp.pad(w, ((0, rows - w.shape[0]), (0, 0)))
    hi = w.astype(BF16)
    return jnp.stack([hi, (w - hi.astype(F32)).astype(BF16)])


def kernel(x, p, norm_mix_pre, norm_mix_post, norm_mlp_pre, norm_mlp_post, w_in, w_in_vmix, mu_shift, mu_shift_vmix, w0, w_decay_up, a0, w_iclr_up, v0, w_vmix_up, w_gate_up, k_k, k_a, r_k, gn_gain, gn_bias, dw_w, dw_b, conv_ln_gain, conv_ln_bias, w_out, w_up, w_down, w_ple, norm_ple, w_ple_gate):
    bsz, seq, d_model = x.shape
    depth = w_in.shape[0]
    d_rwkv = w0.shape[1]
    d_conv = dw_w.shape[2]
    m = bsz * seq
    rkv = 3 * d_rwkv
    low = DECAY_RANK + ICLR_RANK + GATE_RANK
    assert d_rwkv % QW == 0 and seq % CHUNK == 0 and (2 * d_conv) % d_rwkv == 0

    tm = _tile(m, 512, 8)
    tm_in = _tile(m, 1024, 8)
    tn = _tile(2 * d_conv + rkv, 512, LANES)
    tf = _tile(w_up.shape[2], 1024, LANES)
    tp = _tile(seq, 256, CONV_HALO)
    tc = _tile(seq, 256, CONV_HALO)
    col0 = (2 * d_conv) // d_rwkv

    head = jnp.arange(QW) // HEAD_DIM
    e_ones = (head[:, None] == head[None, :]).astype(BF16)
    row = lambda a: a.reshape(1, -1)

    def small_cols(cols, width):
        return jnp.pad(cols, ((0, 0), (0, width - cols.shape[1])))

    x2 = x.reshape(m, d_model)
    v_first = None
    for i in range(depth):
        wi = w_in[i]
        w_main = jnp.concatenate([wi[:, rkv + low:], wi[:, :rkv]], axis=1).astype(BF16)
        d0, d1, d2 = rkv, rkv + DECAY_RANK, rkv + DECAY_RANK + ICLR_RANK
        ws = [small_cols(wi[:, d0:d1], SM_ICLR - SM_DECAY),
              small_cols(wi[:, d1:d2], SM_GATE - SM_ICLR),
              small_cols(wi[:, d2:rkv + low], SM_VMIX - SM_GATE)]
        ms = [small_cols(row(mu_shift[i, d0:d1]), SM_ICLR - SM_DECAY),
              small_cols(row(mu_shift[i, d1:d2]), SM_GATE - SM_ICLR),
              small_cols(row(mu_shift[i, d2:rkv + low]), SM_VMIX - SM_GATE)]
        if i == 0:
            ws.append(jnp.zeros((d_model, SM_WIDTH - SM_VMIX), F32))
            ms.append(jnp.zeros((1, SM_WIDTH - SM_VMIX), F32))
            vmix = None
        else:
            ws.append(small_cols(w_in_vmix[i - 1], SM_WIDTH - SM_VMIX))
            ms.append(small_cols(row(mu_shift_vmix[i - 1]), SM_WIDTH - SM_VMIX))
            vmix = (v_first, row(v0[i - 1]), _up_weight(w_vmix_up[i - 1], SM_WIDTH - SM_VMIX))
        w_small = jnp.concatenate(ws, axis=1).astype(BF16)
        mu_small = jnp.concatenate(ms, axis=1)

        z_main, z_small = _inproj(x2, row(norm_mix_pre[i]), w_main, w_small, tm_in, tn)
        z_main = z_main.reshape(bsz, seq, -1)
        z_small = z_small.reshape(bsz, seq, SM_WIDTH)

        at, rt, bt, kt, v, gl, gate, bonus, *rest = _prep(
            z_main, z_small, col0, row(mu_shift[i, :rkv]), mu_small,
            row(w0[i]), _up_weight(w_decay_up[i], SM_ICLR - SM_DECAY),
            row(a0[i]), _up_weight(w_iclr_up[i], SM_GATE - SM_ICLR),
            _up_weight(w_gate_up[i], SM_VMIX - SM_GATE),
            row(k_k[i]), row(k_a[i]), row(r_k[i]), e_ones, vmix, tp)
        if i == 0:
            v_first = rest[0]
        y_r = _wkv(at, rt, bt, kt, v, gl, gate, bonus, row(gn_gain[i]), row(gn_bias[i]), e_ones,
                   _tile(bsz, WKV_ROWS, 1))
        y_c = _conv(z_main, dw_w[i], row(dw_b[i]), row(conv_ln_gain[i]), row(conv_ln_bias[i]), tc)

        wo = w_out[i].astype(BF16)
        x2 = _outproj(y_r.reshape(m, d_rwkv), y_c.reshape(m, d_conv), x2,
                      wo[:d_rwkv], wo[d_rwkv:], row(norm_mix_post[i]), tm)
        x2 = _mlp(x2, row(norm_mlp_pre[i]), w_up[i].astype(BF16), w_down[i].astype(BF16),
                  row(norm_mlp_post[i]), tm, tf)
        x2 = _ple(x2, p[i].reshape(m, -1), w_ple[i].astype(BF16), row(norm_ple[i]),
                  w_ple_gate[i].astype(BF16), tm)
    return x2.reshape(bsz, seq, d_model)
```

```python
import functools

import jax
import jax.numpy as jnp
from jax import lax
from jax.experimental import pallas as pl
from jax.experimental.pallas import tpu as pltpu

F32 = jnp.float32
BF16 = jnp.bfloat16

HEAD_DIM = 64
DECAY_RANK = 64
ICLR_RANK = 64
VMIX_RANK = 32
GATE_RANK = 160
CONV_WIDTH = 31
RMS_EPS = 1e-6
GN_EPS = 64e-5
LN_EPS = 1e-5

LANES = 128
SUBLANES = 8
CONV_ROWS = 128
STRIP = 256
QUAD = 4
QW = QUAD * HEAD_DIM
CHUNK = 64
WKV_ROWS = 4
SM_DECAY, SM_ICLR, SM_GATE, SM_VMIX, SM_WIDTH = 0, 128, 256, 512, 640
CONV_HALO = 32
VMEM_LIMIT = 56 * 1024 * 1024


def _cparams(*sem):
    return pltpu.CompilerParams(dimension_semantics=sem, vmem_limit_bytes=VMEM_LIMIT)


def _rms(x, g):
    return x * lax.rsqrt(jnp.mean(x * x, axis=-1, keepdims=True) + RMS_EPS) * g


def _split_dot(x, w_bf16):
    hi = x.astype(BF16)
    lo = (x - hi.astype(F32)).astype(BF16)
    n = x.shape[0]
    both = jnp.dot(jnp.concatenate([hi, lo], axis=0), w_bf16, preferred_element_type=F32)
    return both[:n] + both[n:]


def _shift(z, prev_row, mu):
    rows = lax.broadcasted_iota(jnp.int32, z.shape, 0)
    prev = jnp.where(rows == 0, prev_row, pltpu.roll(z, 1, 0))
    return z + (prev - z) * mu


def _prep_kernel(has_vmix, *refs):
    (x_ref, g_ref, w_ref, ws_ref, mu_ref, mus_ref,
     w0_ref, wdu_ref, a0_ref, wiu_ref, wgu_ref, kk_ref, ka_ref, rk_ref, e_ref) = refs[:15]
    refs = refs[15:]
    if has_vmix:
        vf_ref, v0_ref, wvu_ref = refs[:3]
        refs = refs[3:]
    at_out, rt_out, bt_out, kt_out, v_out, gl_out, g_out, bo_out = refs[:8]
    refs = refs[8:]
    if not has_vmix:
        vfirst_out = refs[0]
        refs = refs[1:]
    h_ref, z_ref, carry_ref, carry_s_ref, lw_ref, cum_ref, iclr_ref = refs[:7]
    mix_ref = refs[7] if has_vmix else None

    tp = x_ref.shape[1]
    d = w0_ref.shape[1]
    nq = d // QW

    @pl.when(pl.program_id(1) == 0)
    def _():
        carry_ref[...] = jnp.zeros_like(carry_ref)
        carry_s_ref[...] = jnp.zeros_like(carry_s_ref)

    h_ref[...] = _rms(x_ref[0], g_ref[...]).astype(BF16)

    def project(q, slot):
        z_ref[slot] = jnp.dot(h_ref[...], w_ref[:, q * 3 * QW:(q + 1) * 3 * QW],
                              preferred_element_type=F32)

    project(0, 0)
    zs_raw = jnp.dot(h_ref[...], ws_ref[...], preferred_element_type=F32)
    zs = _shift(zs_raw, carry_s_ref[0:1, :], mus_ref[...])
    carry_s_ref[0:1, :] = zs_raw[tp - 1:tp, :]

    def up(act, w_ref):
        hi = act.astype(BF16)
        lo = (act - hi.astype(F32)).astype(BF16)
        n = act.shape[0]
        both = jnp.dot(jnp.concatenate([hi, lo], axis=0), w_ref[0], preferred_element_type=F32)
        return both[:n] + both[n:] + jnp.dot(hi, w_ref[1], preferred_element_type=F32)

    w_raw = w0_ref[...] + up(jnp.tanh(zs[:, SM_DECAY:SM_ICLR]), wdu_ref)
    lw = -jnp.exp(-0.5) * jax.nn.sigmoid(w_raw)
    lw_ref[...] = lw
    iclr_ref[...] = jax.nn.sigmoid(a0_ref[...] + up(zs[:, SM_ICLR:SM_GATE], wiu_ref))
    g_out[0] = up(jax.nn.sigmoid(zs[:, SM_GATE:SM_VMIX]), wgu_ref)
    if has_vmix:
        mix_ref[...] = jax.nn.sigmoid(v0_ref[...] + up(zs[:, SM_VMIX:SM_WIDTH], wvu_ref))

    ti = lax.broadcasted_iota(jnp.int32, (tp, tp), 0)
    tj = lax.broadcasted_iota(jnp.int32, (tp, tp), 1)
    tri = ((ti >= tj) & (ti // CHUNK == tj // CHUNK)).astype(BF16)
    hi = lw.astype(BF16)
    rest = lw - hi.astype(F32)
    mid = rest.astype(BF16)
    lo = (rest - mid.astype(F32)).astype(BF16)
    cum = (jnp.dot(tri, hi, preferred_element_type=F32) + jnp.dot(tri, mid, preferred_element_type=F32)
           + jnp.dot(tri, lo, preferred_element_type=F32))
    cum_ref[...] = cum
    for j in range(tp // CHUNK):
        last = (j + 1) * CHUNK - 1
        gl_out[0, j] = jnp.exp(cum[last:last + 1, :])

    e = e_ref[...]
    for q in range(nq):
        if q + 1 < nq:
            project(q + 1, (q + 1) % 2)
        qc = slice(q * QW, (q + 1) * QW)
        zc = slice(q * 3 * QW, (q + 1) * 3 * QW)
        z = z_ref[q % 2]
        zq = _shift(z, carry_ref[0:1, zc], mu_ref[:, zc])
        carry_ref[0:1, zc] = z[tp - 1:tp, :]
        r, k, v = zq[:, :QW], zq[:, QW:2 * QW], zq[:, 2 * QW:]
        iclr, lw, cum = iclr_ref[:, qc], lw_ref[:, qc], cum_ref[:, qc]
        if has_vmix:
            v = v + (vf_ref[0, :, qc] - v) * mix_ref[:, qc]
        else:
            vfirst_out[0, :, qc] = v
        v_out[0, :, qc] = v.astype(BF16)
        kk = k * kk_ref[:, qc]
        kk = kk / jnp.maximum(jnp.sqrt(_split_dot(kk * kk, e)), 1e-12)
        k = k * (1.0 + (iclr - 1.0) * ka_ref[:, qc])
        bo_out[0, :, qc] = _split_dot(r * k * rk_ref[:, qc], e) * v
        g_inv = jnp.exp(-cum)
        at_out[0, :, qc] = (-kk * jnp.exp(cum - lw)).astype(BF16)
        rt_out[0, :, qc] = (r * jnp.exp(cum)).astype(BF16)
        bt_out[0, :, qc] = (kk * iclr * g_inv).astype(BF16)
        kt_out[0, :, qc] = (k * g_inv).astype(BF16)


def _quad_columns(r, k, v):
    lead, d = r.shape[:-1], r.shape[-1]
    parts = [t.reshape(*lead, d // QW, QW) for t in (r, k, v)]
    return jnp.stack(parts, axis=-2).reshape(*lead, 3 * d)


def _prep(x3, g, w_rkv, w_small, mu, mus, w0, wdu, a0, wiu, wgu, k_k, k_a, r_k, e_ones, vmix, tp):
    b, s, dm = x3.shape
    d = w0.shape[1]
    has_vmix = vmix is not None
    row = lambda bi, t: (bi, t, 0)

    def const(shape):
        return pl.BlockSpec(shape, lambda bi, t: (0,) * len(shape))

    in_specs = [pl.BlockSpec((1, tp, dm), row), const((1, dm)), const((dm, 3 * d)),
                const((dm, SM_WIDTH)), const((1, 3 * d)), const((1, SM_WIDTH)),
                const((1, d)), const((2, LANES, d)), const((1, d)), const((2, LANES, d)),
                const((2, SM_VMIX - SM_GATE, d)), const((1, d)), const((1, d)), const((1, d)),
                const((QW, QW))]
    args = [x3, g, w_rkv, w_small, mu, mus, w0, wdu, a0, wiu, wgu, k_k, k_a, r_k, e_ones]
    wide = pltpu.VMEM((tp, d), F32)
    scratch = [pltpu.VMEM((tp, dm), BF16), pltpu.VMEM((2, tp, 3 * QW), F32),
               pltpu.VMEM((SUBLANES, 3 * d), F32), pltpu.VMEM((SUBLANES, SM_WIDTH), F32),
               wide, wide, wide]
    if has_vmix:
        scratch.append(wide)
        v_first, v0, wvu = vmix
        in_specs += [pl.BlockSpec((1, tp, d), row), const((1, d)), const((2, LANES, d))]
        args += [v_first, v0, wvu]
    tile = pl.BlockSpec((1, tp, d), row)
    nch = tp // CHUNK
    out_specs = [tile] * 5 + [pl.BlockSpec((1, nch, 1, d), lambda bi, t: (bi, t, 0, 0)), tile, tile]
    out_shape = ([jax.ShapeDtypeStruct((b, s, d), BF16)] * 5
                 + [jax.ShapeDtypeStruct((b, s // CHUNK, 1, d), F32)]
                 + [jax.ShapeDtypeStruct((b, s, d), F32)] * 2)
    if not has_vmix:
        out_specs.append(tile)
        out_shape.append(jax.ShapeDtypeStruct((b, s, d), F32))
    return pl.pallas_call(
        functools.partial(_prep_kernel, has_vmix),
        grid=(b, s // tp),
        in_specs=in_specs,
        out_specs=out_specs,
        out_shape=out_shape,
        scratch_shapes=scratch,
        compiler_params=_cparams("parallel", "arbitrary"),
        name="rwkv_prep",
    )(*args)


def _stack(x, block_diag):
    return jnp.where(block_diag, jnp.concatenate([x] * QUAD, axis=0), jnp.zeros((), x.dtype))


def _nt(a, b):
    return lax.dot_general(a, b, (((1,), (1,)), ((), ())), preferred_element_type=F32)


def _tn(a, b):
    return lax.dot_general(a, b, (((0,), (0,)), ((), ())), preferred_element_type=F32)


def _mm(a, b):
    return jnp.dot(a, b, preferred_element_type=F32)


def _wkv_kernel(at_ref, rt_ref, bt_ref, kt_ref, v_ref, gl_ref, g_ref, bo_ref, gg_ref, gb_ref,
                e_ref, y_ref, s_ref):
    @pl.when(pl.program_id(1) == 0)
    def _():
        s_ref[...] = jnp.zeros_like(s_ref)

    nb, c, d = at_ref.shape
    nq = d // QW
    quads = range(nb * nq)
    ri = lax.broadcasted_iota(jnp.int32, (QUAD * c, QW), 0)
    li = lax.broadcasted_iota(jnp.int32, (QUAD * c, QW), 1)
    ti = lax.broadcasted_iota(jnp.int32, (c, QW), 0)
    si = lax.broadcasted_iota(jnp.int32, (c, QW), 1) % c
    block_diag, strict, incl, eye = ri // c == li // HEAD_DIM, ti > si, ti >= si, ti == si
    bf = lambda t: t.astype(BF16)
    stack = lambda t: _stack(t, block_diag)
    cat = lambda *t: jnp.concatenate(t, axis=0)
    lanes = lambda ref, q: ref[q // nq, :, (q % nq) * QW:(q % nq + 1) * QW]

    bt = [lanes(bt_ref, q) for q in quads]
    kt = [lanes(kt_ref, q) for q in quads]
    vv = [lanes(v_ref, q) for q in quads]
    lhs = [cat(lanes(at_ref, q), lanes(rt_ref, q)) for q in quads]
    gram = [_nt(lhs[q], cat(stack(bt[q]), stack(kt[q]))) for q in quads]
    l_ab = [jnp.where(strict, g[:c, :QW], 0.0) for g in gram]
    tri_k = [bf(cat(jnp.where(strict, g[:c, QW:], 0.0), jnp.where(incl, g[c:, QW:], 0.0)))
             for g in gram]
    p_rb = [bf(jnp.where(incl, g[c:, :QW], 0.0)) for g in gram]

    state = [s_ref[q] for q in quads]
    from_state = [_nt(lhs[q], bf(state[q])) for q in quads]
    from_v = [_mm(tri_k[q], stack(vv[q])) for q in quads]

    t_inv = [jnp.where(eye, 1.0, l) for l in l_ab]
    power = [bf(l) for l in l_ab]
    power = [bf(_mm(p, stack(p))) for p in power]
    for _ in range(c.bit_length() - 3):
        both = [_mm(cat(power[q], bf(t_inv[q])), stack(power[q])) for q in quads]
        power = [bf(b[:c]) for b in both]
        t_inv = [t_inv[q] + both[q][c:] for q in quads]
    t_inv = [t_inv[q] + _mm(bf(t_inv[q]), stack(power[q])) for q in quads]

    u = [bf(_mm(bf(t_inv[q]), stack(bf(from_state[q][:c] + from_v[q][:c])))) for q in quads]
    y = [from_state[q][c:] + from_v[q][c:] + _mm(p_rb[q], stack(u[q])) for q in quads]
    upd = [_tn(cat(u[q], vv[q]), cat(bt[q], kt[q])) for q in quads]
    for q in quads:
        g_last = gl_ref[q // nq, 0, :, (q % nq) * QW:(q % nq + 1) * QW]
        s_ref[q] = (state[q] + jnp.where(block_diag, upd[q], 0.0)) * g_last

    e = e_ref[...]
    inv_n = 1.0 / HEAD_DIM
    mean = [_split_dot(t, e) * inv_n for t in y]
    yc = [y[q] - mean[q] for q in quads]
    var = [_split_dot(t * t, e) * inv_n for t in yc]
    for q in quads:
        bi, sl = q // nq, slice((q % nq) * QW, (q % nq + 1) * QW)
        out = (yc[q] * lax.rsqrt(var[q] + GN_EPS) * gg_ref[:, sl] + gb_ref[:, sl]
               + bo_ref[bi, :, sl]) * g_ref[bi, :, sl]
        y_ref[bi, :, sl] = out.astype(y_ref.dtype)


def _wkv(at, rt, bt, kt, v, gl, gate, bonus, gn_gain, gn_bias, e_ones, nb):
    assert CHUNK == HEAD_DIM
    b, s, d = at.shape
    blk = pl.BlockSpec((nb, CHUNK, d), lambda bi, c: (bi, c, 0))
    vec = pl.BlockSpec((1, d), lambda bi, c: (0, 0))
    return pl.pallas_call(
        _wkv_kernel,
        grid=(b // nb, s // CHUNK),
        in_specs=([blk] * 5 + [pl.BlockSpec((nb, 1, 1, d), lambda bi, c: (bi, c, 0, 0))]
                  + [blk] * 2 + [vec] * 2 + [pl.BlockSpec((QW, QW), lambda bi, c: (0, 0))]),
        out_specs=blk,
        out_shape=jax.ShapeDtypeStruct((b, s, d), BF16),
        scratch_shapes=[pltpu.VMEM((nb * d // QW, QW, QW), F32)],
        compiler_params=_cparams("parallel", "arbitrary"),
        name="wkv",
    )(at, rt, bt, kt, v, gl, gate, bonus, gn_gain, gn_bias, e_ones)


def _conv_kernel(x_ref, g_ref, wp_ref, w_ref, b_ref, lg_ref, lb_ref, y_ref,
                 h_ref, z_ref, u_ref, c_ref, sh_ref):
    tc = x_ref.shape[1]
    d = y_ref.shape[2]
    n_strips = d // STRIP

    @pl.when(pl.program_id(1) == 0)
    def _():
        u_ref[0:CONV_HALO, :] = jnp.zeros((CONV_HALO, d), F32)

    h_ref[...] = _rms(x_ref[0], g_ref[...]).astype(BF16)

    dm = h_ref.shape[1]

    def project(s, slot, part=0, parts=1, after=None):
        cols = pl.ds(pl.multiple_of(s * (2 * STRIP), 2 * STRIP), 2 * STRIP)
        ks = slice(part * dm // parts, (part + 1) * dm // parts)
        lhs = h_ref[:, ks]
        if after is not None:
            bits = pltpu.bitcast(after, jnp.uint32)
            zero = pltpu.bitcast(lax.shift_right_logical(lax.shift_right_logical(bits, jnp.uint32(16)),
                                                         jnp.uint32(16)), F32)
            head = lhs[:2 * SUBLANES, :LANES] + jnp.concatenate([zero, zero], axis=0).astype(BF16)
            lhs = jnp.concatenate(
                [jnp.concatenate([head, lhs[:2 * SUBLANES, LANES:]], axis=1), lhs[2 * SUBLANES:]], axis=0)
        acc = jnp.dot(lhs, wp_ref[ks, cols], preferred_element_type=F32)
        z_ref[slot] = acc if part == 0 else z_ref[slot] + acc

    base = CONV_HALO - (CONV_WIDTH - 1)
    rb = min(tc, CONV_ROWS)
    nt = rb // SUBLANES

    def conv_piece(l0, r0):
        cols = pl.ds(pl.multiple_of(l0, LANES), LANES)
        bias = jnp.broadcast_to(b_ref[:, cols], (SUBLANES, LANES))
        if True:
            acc = [bias] * nt
            for res in range(SUBLANES):
                offs = [o for o in range(base, base + CONV_WIDTH) if o % SUBLANES == res]
                lo, hi = offs[0], offs[-1]
                n = hi - lo + rb
                if res:
                    sh_ref[res, 0:n, :] = u_ref[r0 + lo:r0 + lo + n, cols]
                    src, at = sh_ref.at[res], 0
                else:
                    src, at = u_ref.at[:, cols], r0 + lo
                win = [src[at + i * SUBLANES:at + (i + 1) * SUBLANES, :] for i in range(n // SUBLANES)]
                for o in offs:
                    tap = jnp.broadcast_to(w_ref[o - base:o - base + 1, cols], (SUBLANES, LANES))
                    k = (o - lo) // SUBLANES
                    acc = [acc[i] + win[i + k] * tap for i in range(nt)]
            for i in range(nt):
                c_ref[r0 + i * SUBLANES:r0 + (i + 1) * SUBLANES, cols] = acc[i]
        return acc[-1]

    def strip(s, carry):
        slot = s % 2
        z = z_ref[slot]
        cols = pl.ds(pl.multiple_of(s * STRIP, STRIP), STRIP)
        u_ref[CONV_HALO:CONV_HALO + tc, cols] = z[:, :STRIP] * jax.nn.sigmoid(z[:, STRIP:])
        pieces = [(l0, r0) for l0 in range(0, STRIP, LANES) for r0 in range(0, tc, rb)]
        nxt = jnp.minimum(s + 1, n_strips - 1)
        done = None
        for part, (l0, r0) in enumerate(pieces):
            project(nxt, 1 - slot, part, len(pieces), after=done)
            done = conv_piece(s * STRIP + l0, r0)
        return carry

    project(0, 0)
    lax.fori_loop(0, n_strips, strip, 0)
    acc = c_ref[...]
    mean = jnp.mean(acc, axis=-1, keepdims=True)
    cen = acc - mean
    var = jnp.mean(cen * cen, axis=-1, keepdims=True)
    cf = cen * lax.rsqrt(var + LN_EPS) * lg_ref[...] + lb_ref[...]
    y_ref[0] = (cf * jax.nn.sigmoid(cf)).astype(y_ref.dtype)
    u_ref[0:CONV_HALO, :] = u_ref[tc:tc + CONV_HALO, :]


def _strip_columns(w_lin, w_gate):
    dm, d = w_lin.shape
    both = jnp.stack([w_lin.reshape(dm, d // STRIP, STRIP), w_gate.reshape(dm, d // STRIP, STRIP)], axis=2)
    return both.reshape(dm, 2 * d)


def _conv(x3, g, w_proj, dw_w, dw_b, ln_g, ln_b, tc):
    b, s, dm = x3.shape
    d = dw_w.shape[1]
    const = lambda shape: pl.BlockSpec(shape, lambda bi, t: (0, 0))
    return pl.pallas_call(
        _conv_kernel,
        grid=(b, s // tc),
        in_specs=[pl.BlockSpec((1, tc, dm), lambda bi, t: (bi, t, 0)), const((1, dm)),
                  const((dm, 2 * d)),
                  const((CONV_WIDTH, d)), const((1, d)), const((1, d)), const((1, d))],
        out_specs=pl.BlockSpec((1, tc, d), lambda bi, t: (bi, t, 0)),
        out_shape=jax.ShapeDtypeStruct((b, s, d), BF16),
        scratch_shapes=[pltpu.VMEM((tc, dm), BF16), pltpu.VMEM((2, tc, 2 * STRIP), F32),
                        pltpu.VMEM((tc + CONV_HALO, d), F32), pltpu.VMEM((tc, d), F32),
                        pltpu.VMEM((SUBLANES, min(tc, CONV_ROWS) + CONV_HALO, LANES), F32)],
        compiler_params=_cparams("parallel", "arbitrary"),
        name="conformer_conv",
    )(x3, g, w_proj, dw_w, dw_b, ln_g, ln_b)


def _outproj_kernel(yr_ref, yc_ref, x_ref, wr_ref, wc_ref, g_ref, o_ref):
    mixed = (jnp.dot(yr_ref[...], wr_ref[...], preferred_element_type=F32)
             + jnp.dot(yc_ref[...], wc_ref[...], preferred_element_type=F32))
    o_ref[...] = x_ref[...] + _rms(mixed, g_ref[...])


def _outproj(y_r, y_c, x2, w_r, w_c, g, tm):
    m, d = x2.shape
    dr, dc = y_r.shape[1], y_c.shape[1]
    return pl.pallas_call(
        _outproj_kernel,
        grid=(m // tm,),
        in_specs=[pl.BlockSpec((tm, dr), lambda i: (i, 0)),
                  pl.BlockSpec((tm, dc), lambda i: (i, 0)),
                  pl.BlockSpec((tm, d), lambda i: (i, 0)),
                  pl.BlockSpec((dr, d), lambda i: (0, 0)),
                  pl.BlockSpec((dc, d), lambda i: (0, 0)),
                  pl.BlockSpec((1, d), lambda i: (0, 0))],
        out_specs=pl.BlockSpec((tm, d), lambda i: (i, 0)),
        out_shape=jax.ShapeDtypeStruct((m, d), F32),
        compiler_params=_cparams("parallel"),
        name="outproj",
    )(y_r, y_c, x2, w_r, w_c, g)


def _mlp_kernel(x_ref, g1_ref, wu_ref, wd_ref, g2_ref, o_ref, h_ref, acc_ref):
    j = pl.program_id(1)

    @pl.when(j == 0)
    def _():
        h_ref[...] = _rms(x_ref[...], g1_ref[...]).astype(BF16)
        acc_ref[...] = jnp.zeros_like(acc_ref)

    a = jnp.maximum(jnp.dot(h_ref[...], wu_ref[...], preferred_element_type=F32), 0.0)
    acc_ref[...] += jnp.dot((a * a).astype(BF16), wd_ref[...], preferred_element_type=F32)

    @pl.when(j == pl.num_programs(1) - 1)
    def _():
        o_ref[...] = x_ref[...] + _rms(acc_ref[...], g2_ref[...])


def _mlp(x2, g1, w_up, w_down, g2, tm, tf):
    m, d = x2.shape
    f = w_up.shape[1]
    return pl.pallas_call(
        _mlp_kernel,
        grid=(m // tm, f // tf),
        in_specs=[pl.BlockSpec((tm, d), lambda i, j: (i, 0)),
                  pl.BlockSpec((1, d), lambda i, j: (0, 0)),
                  pl.BlockSpec((d, tf), lambda i, j: (0, j)),
                  pl.BlockSpec((tf, d), lambda i, j: (j, 0)),
                  pl.BlockSpec((1, d), lambda i, j: (0, 0))],
        out_specs=pl.BlockSpec((tm, d), lambda i, j: (i, 0)),
        out_shape=jax.ShapeDtypeStruct((m, d), F32),
        scratch_shapes=[pltpu.VMEM((tm, d), BF16), pltpu.VMEM((tm, d), F32)],
        compiler_params=_cparams("parallel", "arbitrary"),
        name="mlp",
    )(x2, g1, w_up, w_down, g2)


def _ple_kernel(x_ref, p_ref, wp_ref, g_ref, wg_ref, o_ref):
    x = x_ref[...]
    e = _rms(jnp.dot(p_ref[...].astype(BF16), wp_ref[...], preferred_element_type=F32), g_ref[...])
    gate = jax.nn.sigmoid(jnp.dot(x.astype(BF16), wg_ref[...], preferred_element_type=F32))
    o_ref[...] = x + gate * e


def _ple(x2, p2, w_ple, g, w_gate, tm):
    m, d = x2.shape
    dp = p2.shape[1]
    return pl.pallas_call(
        _ple_kernel,
        grid=(m // tm,),
        in_specs=[pl.BlockSpec((tm, d), lambda i: (i, 0)),
                  pl.BlockSpec((tm, dp), lambda i: (i, 0)),
                  pl.BlockSpec((dp, d), lambda i: (0, 0)),
                  pl.BlockSpec((1, d), lambda i: (0, 0)),
                  pl.BlockSpec((d, d), lambda i: (0, 0))],
        out_specs=pl.BlockSpec((tm, d), lambda i: (i, 0)),
        out_shape=jax.ShapeDtypeStruct((m, d), F32),
        compiler_params=_cparams("parallel"),
        name="ple",
    )(x2, p2, w_ple, g, w_gate)


def _tile(n, want, align):
    t = (min(n, want) // align) * align
    while n % t:
        t -= align
    return t


def _up_weight(w, rows):
    w = jnp.pad(w, ((0, rows - w.shape[0]), (0, 0)))
    hi = w.astype(BF16)
    return jnp.stack([hi, (w - hi.astype(F32)).astype(BF16)])


def kernel(x, p, norm_mix_pre, norm_mix_post, norm_mlp_pre, norm_mlp_post, w_in, w_in_vmix, mu_shift, mu_shift_vmix, w0, w_decay_up, a0, w_iclr_up, v0, w_vmix_up, w_gate_up, k_k, k_a, r_k, gn_gain, gn_bias, dw_w, dw_b, conv_ln_gain, conv_ln_bias, w_out, w_up, w_down, w_ple, norm_ple, w_ple_gate):
    bsz, seq, d_model = x.shape
    depth = w_in.shape[0]
    d_rwkv = w0.shape[1]
    d_conv = dw_w.shape[2]
    m = bsz * seq
    rkv = 3 * d_rwkv
    low = DECAY_RANK + ICLR_RANK + GATE_RANK
    assert d_rwkv % QW == 0 and seq % CHUNK == 0 and d_conv % STRIP == 0

    tm = _tile(m, 512, 8)
    tf = _tile(w_up.shape[2], 1024, LANES)
    tp = _tile(seq, 256, CHUNK)
    tc = _tile(seq, 256, CONV_HALO)

    head = jnp.arange(QW) // HEAD_DIM
    e_ones = (head[:, None] == head[None, :]).astype(BF16)
    row = lambda a: a.reshape(1, -1)

    def small_cols(cols, width):
        return jnp.pad(cols, ((0, 0), (0, width - cols.shape[1])))

    x2 = x.reshape(m, d_model)
    v_first = None
    for i in range(depth):
        wi = w_in[i]
        w_rkv = _quad_columns(wi[:, :d_rwkv], wi[:, d_rwkv:2 * d_rwkv], wi[:, 2 * d_rwkv:rkv]).astype(BF16)
        mu_rkv = _quad_columns(row(mu_shift[i, :d_rwkv]), row(mu_shift[i, d_rwkv:2 * d_rwkv]),
                               row(mu_shift[i, 2 * d_rwkv:rkv]))
        w_glu = _strip_columns(wi[:, rkv + low:rkv + low + d_conv], wi[:, rkv + low + d_conv:]).astype(BF16)
        d0, d1, d2 = rkv, rkv + DECAY_RANK, rkv + DECAY_RANK + ICLR_RANK
        ws = [small_cols(wi[:, d0:d1], SM_ICLR - SM_DECAY),
              small_cols(wi[:, d1:d2], SM_GATE - SM_ICLR),
              small_cols(wi[:, d2:rkv + low], SM_VMIX - SM_GATE)]
        ms = [small_cols(row(mu_shift[i, d0:d1]), SM_ICLR - SM_DECAY),
              small_cols(row(mu_shift[i, d1:d2]), SM_GATE - SM_ICLR),
              small_cols(row(mu_shift[i, d2:rkv + low]), SM_VMIX - SM_GATE)]
        if i == 0:
            ws.append(jnp.zeros((d_model, SM_WIDTH - SM_VMIX), F32))
            ms.append(jnp.zeros((1, SM_WIDTH - SM_VMIX), F32))
            vmix = None
        else:
            ws.append(small_cols(w_in_vmix[i - 1], SM_WIDTH - SM_VMIX))
            ms.append(small_cols(row(mu_shift_vmix[i - 1]), SM_WIDTH - SM_VMIX))
            vmix = (v_first, row(v0[i - 1]), _up_weight(w_vmix_up[i - 1], SM_WIDTH - SM_VMIX))
        w_small = jnp.concatenate(ws, axis=1).astype(BF16)
        mu_small = jnp.concatenate(ms, axis=1)

        x3 = x2.reshape(bsz, seq, d_model)
        g_pre = row(norm_mix_pre[i])
        at, rt, bt, kt, v, gl, gate, bonus, *rest = _prep(
            x3, g_pre, w_rkv, w_small, mu_rkv, mu_small,
            row(w0[i]), _up_weight(w_decay_up[i], SM_ICLR - SM_DECAY),
            row(a0[i]), _up_weight(w_iclr_up[i], SM_GATE - SM_ICLR),
            _up_weight(w_gate_up[i], SM_VMIX - SM_GATE),
            row(k_k[i]), row(k_a[i]), row(r_k[i]), e_ones, vmix, tp)
        if i == 0:
            v_first = rest[0]
        y_r = _wkv(at, rt, bt, kt, v, gl, gate, bonus, row(gn_gain[i]), row(gn_bias[i]), e_ones,
                   _tile(bsz, WKV_ROWS, 1))
        y_c = _conv(x3, g_pre, w_glu, dw_w[i], row(dw_b[i]), row(conv_ln_gain[i]),
                    row(conv_ln_bias[i]), tc)

        wo = w_out[i].astype(BF16)
        x2 = _outproj(y_r.reshape(m, d_rwkv), y_c.reshape(m, d_conv), x2,
                      wo[:d_rwkv], wo[d_rwkv:], row(norm_mix_post[i]), tm)
        x2 = _mlp(x2, row(norm_mlp_pre[i]), w_up[i].astype(BF16), w_down[i].astype(BF16),
                  row(norm_mlp_post[i]), tm, tf)
        x2 = _ple(x2, p[i].reshape(m, -1), w_ple[i].astype(BF16), row(norm_ple[i]),
                  w_ple_gate[i].astype(BF16), tm)
    return x2.reshape(bsz, seq, d_model)
```

```python
import functools

import jax
import jax.numpy as jnp
from jax import lax
from jax.experimental import pallas as pl
from jax.experimental.pallas import tpu as pltpu

F32 = jnp.float32
BF16 = jnp.bfloat16

HEAD_DIM = 64
DECAY_RANK = 64
ICLR_RANK = 64
VMIX_RANK = 32
GATE_RANK = 160
CONV_WIDTH = 31
RMS_EPS = 1e-6
GN_EPS = 64e-5
LN_EPS = 1e-5

LANES = 128
SUBLANES = 8
CONV_ROWS = 128
QUAD = 4
QW = QUAD * HEAD_DIM
CHUNK = 64
WKV_ROWS = 4
SM_DECAY, SM_ICLR, SM_GATE, SM_VMIX, SM_WIDTH = 0, 128, 256, 512, 640
CONV_HALO = 32
VMEM_LIMIT = 56 * 1024 * 1024


def _cparams(*sem):
    return pltpu.CompilerParams(dimension_semantics=sem, vmem_limit_bytes=VMEM_LIMIT)


def _rms(x, g):
    return x * lax.rsqrt(jnp.mean(x * x, axis=-1, keepdims=True) + RMS_EPS) * g


def _split_dot(x, w_bf16):
    hi = x.astype(BF16)
    lo = (x - hi.astype(F32)).astype(BF16)
    n = x.shape[0]
    both = jnp.dot(jnp.concatenate([hi, lo], axis=0), w_bf16, preferred_element_type=F32)
    return both[:n] + both[n:]


def _shift(z, prev_row, mu):
    rows = lax.broadcasted_iota(jnp.int32, z.shape, 0)
    prev = jnp.where(rows == 0, prev_row, pltpu.roll(z, 1, 0))
    return z + (prev - z) * mu


def _prep_kernel(has_vmix, *refs):
    (x_ref, g_ref, w_ref, ws_ref, wglu_ref, mu_ref, mus_ref,
     w0_ref, wdu_ref, a0_ref, wiu_ref, wgu_ref, kk_ref, ka_ref, rk_ref, e_ref) = refs[:16]
    refs = refs[16:]
    if has_vmix:
        vf_ref, v0_ref, wvu_ref = refs[:3]
        refs = refs[3:]
    at_out, rt_out, bt_out, kt_out, v_out, gl_out, g_out, bo_out, zglu_out = refs[:9]
    refs = refs[9:]
    if not has_vmix:
        vfirst_out = refs[0]
        refs = refs[1:]
    h_ref, z_ref, carry_ref, carry_s_ref, lw_ref, cum_ref, iclr_ref = refs[:7]
    mix_ref = refs[7] if has_vmix else None

    tp = x_ref.shape[1]
    d = w0_ref.shape[1]
    nq = d // QW

    @pl.when(pl.program_id(1) == 0)
    def _():
        carry_ref[...] = jnp.zeros_like(carry_ref)
        carry_s_ref[...] = jnp.zeros_like(carry_s_ref)

    h_ref[...] = _rms(x_ref[0], g_ref[...]).astype(BF16)

    def project(q, slot):
        z_ref[slot] = jnp.dot(h_ref[...], w_ref[:, q * 3 * QW:(q + 1) * 3 * QW],
                              preferred_element_type=F32)

    project(0, 0)
    zs_raw = jnp.dot(h_ref[...], ws_ref[...], preferred_element_type=F32)
    zs = _shift(zs_raw, carry_s_ref[0:1, :], mus_ref[...])
    carry_s_ref[0:1, :] = zs_raw[tp - 1:tp, :]

    def up(act, w_ref):
        hi = act.astype(BF16)
        lo = (act - hi.astype(F32)).astype(BF16)
        n = act.shape[0]
        both = jnp.dot(jnp.concatenate([hi, lo], axis=0), w_ref[0], preferred_element_type=F32)
        return both[:n] + both[n:] + jnp.dot(hi, w_ref[1], preferred_element_type=F32)

    w_raw = w0_ref[...] + up(jnp.tanh(zs[:, SM_DECAY:SM_ICLR]), wdu_ref)
    lw = -jnp.exp(-0.5) * jax.nn.sigmoid(w_raw)
    lw_ref[...] = lw
    iclr_ref[...] = jax.nn.sigmoid(a0_ref[...] + up(zs[:, SM_ICLR:SM_GATE], wiu_ref))
    g_out[0] = up(jax.nn.sigmoid(zs[:, SM_GATE:SM_VMIX]), wgu_ref)
    if has_vmix:
        mix_ref[...] = jax.nn.sigmoid(v0_ref[...] + up(zs[:, SM_VMIX:SM_WIDTH], wvu_ref))

    ti = lax.broadcasted_iota(jnp.int32, (tp, tp), 0)
    tj = lax.broadcasted_iota(jnp.int32, (tp, tp), 1)
    tri = ((ti >= tj) & (ti // CHUNK == tj // CHUNK)).astype(BF16)
    hi = lw.astype(BF16)
    rest = lw - hi.astype(F32)
    mid = rest.astype(BF16)
    lo = (rest - mid.astype(F32)).astype(BF16)
    cum = (jnp.dot(tri, hi, preferred_element_type=F32) + jnp.dot(tri, mid, preferred_element_type=F32)
           + jnp.dot(tri, lo, preferred_element_type=F32))
    cum_ref[...] = cum
    for j in range(tp // CHUNK):
        last = (j + 1) * CHUNK - 1
        gl_out[0, j] = jnp.exp(cum[last:last + 1, :])

    e = e_ref[...]
    gw = wglu_ref.shape[1] // nq
    for q in range(nq):
        if q + 1 < nq:
            project(q + 1, (q + 1) % 2)
        zglu_out[0, :, q * gw:(q + 1) * gw] = jnp.dot(
            h_ref[...], wglu_ref[:, q * gw:(q + 1) * gw], preferred_element_type=F32)
        qc = slice(q * QW, (q + 1) * QW)
        zc = slice(q * 3 * QW, (q + 1) * 3 * QW)
        z = z_ref[q % 2]
        zq = _shift(z, carry_ref[0:1, zc], mu_ref[:, zc])
        carry_ref[0:1, zc] = z[tp - 1:tp, :]
        r, k, v = zq[:, :QW], zq[:, QW:2 * QW], zq[:, 2 * QW:]
        iclr, lw, cum = iclr_ref[:, qc], lw_ref[:, qc], cum_ref[:, qc]
        if has_vmix:
            v = v + (vf_ref[0, :, qc] - v) * mix_ref[:, qc]
        else:
            vfirst_out[0, :, qc] = v
        v_out[0, :, qc] = v.astype(BF16)
        kk = k * kk_ref[:, qc]
        kk = kk / jnp.maximum(jnp.sqrt(_split_dot(kk * kk, e)), 1e-12)
        k = k * (1.0 + (iclr - 1.0) * ka_ref[:, qc])
        bo_out[0, :, qc] = _split_dot(r * k * rk_ref[:, qc], e) * v
        g_inv = jnp.exp(-cum)
        at_out[0, :, qc] = (-kk * jnp.exp(cum - lw)).astype(BF16)
        rt_out[0, :, qc] = (r * jnp.exp(cum)).astype(BF16)
        bt_out[0, :, qc] = (kk * iclr * g_inv).astype(BF16)
        kt_out[0, :, qc] = (k * g_inv).astype(BF16)


def _quad_columns(r, k, v):
    lead, d = r.shape[:-1], r.shape[-1]
    parts = [t.reshape(*lead, d // QW, QW) for t in (r, k, v)]
    return jnp.stack(parts, axis=-2).reshape(*lead, 3 * d)


def _prep(x3, g, w_rkv, w_small, w_glu, mu, mus, w0, wdu, a0, wiu, wgu, k_k, k_a, r_k, e_ones,
          vmix, tp):
    b, s, dm = x3.shape
    d = w0.shape[1]
    dg = w_glu.shape[1]
    has_vmix = vmix is not None
    row = lambda bi, t: (bi, t, 0)

    def const(shape):
        return pl.BlockSpec(shape, lambda bi, t: (0,) * len(shape))

    in_specs = [pl.BlockSpec((1, tp, dm), row), const((1, dm)), const((dm, 3 * d)),
                const((dm, SM_WIDTH)), const((dm, dg)), const((1, 3 * d)), const((1, SM_WIDTH)),
                const((1, d)), const((2, LANES, d)), const((1, d)), const((2, LANES, d)),
                const((2, SM_VMIX - SM_GATE, d)), const((1, d)), const((1, d)), const((1, d)),
                const((QW, QW))]
    args = [x3, g, w_rkv, w_small, w_glu, mu, mus, w0, wdu, a0, wiu, wgu, k_k, k_a, r_k, e_ones]
    wide = pltpu.VMEM((tp, d), F32)
    scratch = [pltpu.VMEM((tp, dm), BF16), pltpu.VMEM((2, tp, 3 * QW), F32),
               pltpu.VMEM((SUBLANES, 3 * d), F32), pltpu.VMEM((SUBLANES, SM_WIDTH), F32),
               wide, wide, wide]
    if has_vmix:
        scratch.append(wide)
        v_first, v0, wvu = vmix
        in_specs += [pl.BlockSpec((1, tp, d), row), const((1, d)), const((2, LANES, d))]
        args += [v_first, v0, wvu]
    tile = pl.BlockSpec((1, tp, d), row)
    nch = tp // CHUNK
    out_specs = ([tile] * 5 + [pl.BlockSpec((1, nch, 1, d), lambda bi, t: (bi, t, 0, 0)), tile, tile]
                 + [pl.BlockSpec((1, tp, dg), row)])
    out_shape = ([jax.ShapeDtypeStruct((b, s, d), BF16)] * 5
                 + [jax.ShapeDtypeStruct((b, s // CHUNK, 1, d), F32)]
                 + [jax.ShapeDtypeStruct((b, s, d), F32)] * 2
                 + [jax.ShapeDtypeStruct((b, s, dg), F32)])
    if not has_vmix:
        out_specs.append(tile)
        out_shape.append(jax.ShapeDtypeStruct((b, s, d), F32))
    return pl.pallas_call(
        functools.partial(_prep_kernel, has_vmix),
        grid=(b, s // tp),
        in_specs=in_specs,
        out_specs=out_specs,
        out_shape=out_shape,
        scratch_shapes=scratch,
        compiler_params=_cparams("parallel", "arbitrary"),
        name="rwkv_prep",
    )(*args)


def _stack(x, block_diag):
    return jnp.where(block_diag, jnp.concatenate([x] * QUAD, axis=0), jnp.zeros((), x.dtype))


def _nt(a, b):
    return lax.dot_general(a, b, (((1,), (1,)), ((), ())), preferred_element_type=F32)


def _tn(a, b):
    return lax.dot_general(a, b, (((0,), (0,)), ((), ())), preferred_element_type=F32)


def _mm(a, b):
    return jnp.dot(a, b, preferred_element_type=F32)


def _wkv_kernel(at_ref, rt_ref, bt_ref, kt_ref, v_ref, gl_ref, g_ref, bo_ref, gg_ref, gb_ref,
                e_ref, y_ref, s_ref):
    @pl.when(pl.program_id(1) == 0)
    def _():
        s_ref[...] = jnp.zeros_like(s_ref)

    nb, c, d = at_ref.shape
    nq = d // QW
    quads = range(nb * nq)
    ri = lax.broadcasted_iota(jnp.int32, (QUAD * c, QW), 0)
    li = lax.broadcasted_iota(jnp.int32, (QUAD * c, QW), 1)
    ti = lax.broadcasted_iota(jnp.int32, (c, QW), 0)
    si = lax.broadcasted_iota(jnp.int32, (c, QW), 1) % c
    block_diag, strict, incl, eye = ri // c == li // HEAD_DIM, ti > si, ti >= si, ti == si
    bf = lambda t: t.astype(BF16)
    stack = lambda t: _stack(t, block_diag)
    cat = lambda *t: jnp.concatenate(t, axis=0)
    lanes = lambda ref, q: ref[q // nq, :, (q % nq) * QW:(q % nq + 1) * QW]

    bt = [lanes(bt_ref, q) for q in quads]
    kt = [lanes(kt_ref, q) for q in quads]
    vv = [lanes(v_ref, q) for q in quads]
    lhs = [cat(lanes(at_ref, q), lanes(rt_ref, q)) for q in quads]
    gram = [_nt(lhs[q], cat(stack(bt[q]), stack(kt[q]))) for q in quads]
    l_ab = [jnp.where(strict, g[:c, :QW], 0.0) for g in gram]
    tri_k = [bf(cat(jnp.where(strict, g[:c, QW:], 0.0), jnp.where(incl, g[c:, QW:], 0.0)))
             for g in gram]
    p_rb = [bf(jnp.where(incl, g[c:, :QW], 0.0)) for g in gram]

    state = [s_ref[q] for q in quads]
    from_state = [_nt(lhs[q], bf(state[q])) for q in quads]
    from_v = [_mm(tri_k[q], stack(vv[q])) for q in quads]

    t_inv = [jnp.where(eye, 1.0, l) for l in l_ab]
    power = [bf(l) for l in l_ab]
    power = [bf(_mm(p, stack(p))) for p in power]
    for _ in range(c.bit_length() - 3):
        both = [_mm(cat(power[q], bf(t_inv[q])), stack(power[q])) for q in quads]
        power = [bf(b[:c]) for b in both]
        t_inv = [t_inv[q] + both[q][c:] for q in quads]
    t_inv = [t_inv[q] + _mm(bf(t_inv[q]), stack(power[q])) for q in quads]

    u = [bf(_mm(bf(t_inv[q]), stack(bf(from_state[q][:c] + from_v[q][:c])))) for q in quads]
    y = [from_state[q][c:] + from_v[q][c:] + _mm(p_rb[q], stack(u[q])) for q in quads]
    upd = [_tn(cat(u[q], vv[q]), cat(bt[q], kt[q])) for q in quads]
    for q in quads:
        g_last = gl_ref[q // nq, 0, :, (q % nq) * QW:(q % nq + 1) * QW]
        s_ref[q] = (state[q] + jnp.where(block_diag, upd[q], 0.0)) * g_last

    e = e_ref[...]
    inv_n = 1.0 / HEAD_DIM
    mean = [_split_dot(t, e) * inv_n for t in y]
    yc = [y[q] - mean[q] for q in quads]
    var = [_split_dot(t * t, e) * inv_n for t in yc]
    for q in quads:
        bi, sl = q // nq, slice((q % nq) * QW, (q % nq + 1) * QW)
        out = (yc[q] * lax.rsqrt(var[q] + GN_EPS) * gg_ref[:, sl] + gb_ref[:, sl]
               + bo_ref[bi, :, sl]) * g_ref[bi, :, sl]
        y_ref[bi, :, sl] = out.astype(y_ref.dtype)


def _wkv(at, rt, bt, kt, v, gl, gate, bonus, gn_gain, gn_bias, e_ones, nb):
    assert CHUNK == HEAD_DIM
    b, s, d = at.shape
    blk = pl.BlockSpec((nb, CHUNK, d), lambda bi, c: (bi, c, 0))
    vec = pl.BlockSpec((1, d), lambda bi, c: (0, 0))
    return pl.pallas_call(
        _wkv_kernel,
        grid=(b // nb, s // CHUNK),
        in_specs=([blk] * 5 + [pl.BlockSpec((nb, 1, 1, d), lambda bi, c: (bi, c, 0, 0))]
                  + [blk] * 2 + [vec] * 2 + [pl.BlockSpec((QW, QW), lambda bi, c: (0, 0))]),
        out_specs=blk,
        out_shape=jax.ShapeDtypeStruct((b, s, d), BF16),
        scratch_shapes=[pltpu.VMEM((nb * d // QW, QW, QW), F32)],
        compiler_params=_cparams("parallel", "arbitrary"),
        name="wkv",
    )(at, rt, bt, kt, v, gl, gate, bonus, gn_gain, gn_bias, e_ones)


def _conv_kernel(z_ref, w_ref, b_ref, lg_ref, lb_ref, y_ref, u_ref, c_ref, sh_ref):
    tc = z_ref.shape[1]
    d = y_ref.shape[2]

    @pl.when(pl.program_id(1) == 0)
    def _():
        u_ref[0:CONV_HALO, :] = jnp.zeros((CONV_HALO, d), F32)

    z = z_ref[0]
    u_ref[CONV_HALO:CONV_HALO + tc, :] = z[:, :d] * jax.nn.sigmoid(z[:, d:])

    base = CONV_HALO - (CONV_WIDTH - 1)
    rb = min(tc, CONV_ROWS)
    nt = rb // SUBLANES

    def strip(s, carry):
        cols = pl.ds(pl.multiple_of(s * LANES, LANES), LANES)
        bias = jnp.broadcast_to(b_ref[:, cols], (SUBLANES, LANES))
        for r0 in range(0, tc, rb):
            acc = [bias] * nt
            for res in range(SUBLANES):
                offs = [o for o in range(base, base + CONV_WIDTH) if o % SUBLANES == res]
                lo, hi = offs[0], offs[-1]
                n = hi - lo + rb
                if res:
                    sh_ref[res, 0:n, :] = u_ref[r0 + lo:r0 + lo + n, cols]
                    src, at = sh_ref.at[res], 0
                else:
                    src, at = u_ref.at[:, cols], r0 + lo
                win = [src[at + i * SUBLANES:at + (i + 1) * SUBLANES, :] for i in range(n // SUBLANES)]
                for o in offs:
                    tap = jnp.broadcast_to(w_ref[o - base:o - base + 1, cols], (SUBLANES, LANES))
                    k = (o - lo) // SUBLANES
                    acc = [acc[i] + win[i + k] * tap for i in range(nt)]
            for i in range(nt):
                c_ref[r0 + i * SUBLANES:r0 + (i + 1) * SUBLANES, cols] = acc[i]
        return carry

    lax.fori_loop(0, d // LANES, strip, 0)
    acc = c_ref[...]
    mean = jnp.mean(acc, axis=-1, keepdims=True)
    cen = acc - mean
    var = jnp.mean(cen * cen, axis=-1, keepdims=True)
    cf = cen * lax.rsqrt(var + LN_EPS) * lg_ref[...] + lb_ref[...]
    y_ref[0] = (cf * jax.nn.sigmoid(cf)).astype(y_ref.dtype)
    u_ref[0:CONV_HALO, :] = u_ref[tc:tc + CONV_HALO, :]


def _conv(z_glu, dw_w, dw_b, ln_g, ln_b, tc):
    b, s, _ = z_glu.shape
    d = dw_w.shape[1]
    const = lambda shape: pl.BlockSpec(shape, lambda bi, t: (0, 0))
    return pl.pallas_call(
        _conv_kernel,
        grid=(b, s // tc),
        in_specs=[pl.BlockSpec((1, tc, 2 * d), lambda bi, t: (bi, t, 0)),
                  const((CONV_WIDTH, d)), const((1, d)), const((1, d)), const((1, d))],
        out_specs=pl.BlockSpec((1, tc, d), lambda bi, t: (bi, t, 0)),
        out_shape=jax.ShapeDtypeStruct((b, s, d), BF16),
        scratch_shapes=[pltpu.VMEM((tc + CONV_HALO, d), F32), pltpu.VMEM((tc, d), F32),
                        pltpu.VMEM((SUBLANES, min(tc, CONV_ROWS) + CONV_HALO, LANES), F32)],
        compiler_params=_cparams("parallel", "arbitrary"),
        name="conformer_conv",
    )(z_glu, dw_w, dw_b, ln_g, ln_b)


def _outproj_kernel(yr_ref, yc_ref, x_ref, wr_ref, wc_ref, g_ref, o_ref):
    mixed = (jnp.dot(yr_ref[...], wr_ref[...], preferred_element_type=F32)
             + jnp.dot(yc_ref[...], wc_ref[...], preferred_element_type=F32))
    o_ref[...] = x_ref[...] + _rms(mixed, g_ref[...])


def _outproj(y_r, y_c, x2, w_r, w_c, g, tm):
    m, d = x2.shape
    dr, dc = y_r.shape[1], y_c.shape[1]
    return pl.pallas_call(
        _outproj_kernel,
        grid=(m // tm,),
        in_specs=[pl.BlockSpec((tm, dr), lambda i: (i, 0)),
                  pl.BlockSpec((tm, dc), lambda i: (i, 0)),
                  pl.BlockSpec((tm, d), lambda i: (i, 0)),
                  pl.BlockSpec((dr, d), lambda i: (0, 0)),
                  pl.BlockSpec((dc, d), lambda i: (0, 0)),
                  pl.BlockSpec((1, d), lambda i: (0, 0))],
        out_specs=pl.BlockSpec((tm, d), lambda i: (i, 0)),
        out_shape=jax.ShapeDtypeStruct((m, d), F32),
        compiler_params=_cparams("parallel"),
        name="outproj",
    )(y_r, y_c, x2, w_r, w_c, g)


def _mlp_kernel(x_ref, g1_ref, wu_ref, wd_ref, g2_ref, o_ref, h_ref, acc_ref):
    j = pl.program_id(1)

    @pl.when(j == 0)
    def _():
        h_ref[...] = _rms(x_ref[...], g1_ref[...]).astype(BF16)
        acc_ref[...] = jnp.zeros_like(acc_ref)

    a = jnp.maximum(jnp.dot(h_ref[...], wu_ref[...], preferred_element_type=F32), 0.0)
    acc_ref[...] += jnp.dot((a * a).astype(BF16), wd_ref[...], preferred_element_type=F32)

    @pl.when(j == pl.num_programs(1) - 1)
    def _():
        o_ref[...] = x_ref[...] + _rms(acc_ref[...], g2_ref[...])


def _mlp(x2, g1, w_up, w_down, g2, tm, tf):
    m, d = x2.shape
    f = w_up.shape[1]
    return pl.pallas_call(
        _mlp_kernel,
        grid=(m // tm, f // tf),
        in_specs=[pl.BlockSpec((tm, d), lambda i, j: (i, 0)),
                  pl.BlockSpec((1, d), lambda i, j: (0, 0)),
                  pl.BlockSpec((d, tf), lambda i, j: (0, j)),
                  pl.BlockSpec((tf, d), lambda i, j: (j, 0)),
                  pl.BlockSpec((1, d), lambda i, j: (0, 0))],
        out_specs=pl.BlockSpec((tm, d), lambda i, j: (i, 0)),
        out_shape=jax.ShapeDtypeStruct((m, d), F32),
        scratch_shapes=[pltpu.VMEM((tm, d), BF16), pltpu.VMEM((tm, d), F32)],
        compiler_params=_cparams("parallel", "arbitrary"),
        name="mlp",
    )(x2, g1, w_up, w_down, g2)


def _ple_kernel(x_ref, p_ref, wp_ref, g_ref, wg_ref, o_ref):
    x = x_ref[...]
    e = _rms(jnp.dot(p_ref[...].astype(BF16), wp_ref[...], preferred_element_type=F32), g_ref[...])
    gate = jax.nn.sigmoid(jnp.dot(x.astype(BF16), wg_ref[...], preferred_element_type=F32))
    o_ref[...] = x + gate * e


def _ple(x2, p2, w_ple, g, w_gate, tm):
    m, d = x2.shape
    dp = p2.shape[1]
    return pl.pallas_call(
        _ple_kernel,
        grid=(m // tm,),
        in_specs=[pl.BlockSpec((tm, d), lambda i: (i, 0)),
                  pl.BlockSpec((tm, dp), lambda i: (i, 0)),
                  pl.BlockSpec((dp, d), lambda i: (0, 0)),
                  pl.BlockSpec((1, d), lambda i: (0, 0)),
                  pl.BlockSpec((d, d), lambda i: (0, 0))],
        out_specs=pl.BlockSpec((tm, d), lambda i: (i, 0)),
        out_shape=jax.ShapeDtypeStruct((m, d), F32),
        compiler_params=_cparams("parallel"),
        name="ple",
    )(x2, p2, w_ple, g, w_gate)


def _tile(n, want, align):
    t = (min(n, want) // align) * align
    while n % t:
        t -= align
    return t


def _up_weight(w, rows):
    w = jnp.pad(w, ((0, rows - w.shape[0]), (0, 0)))
    hi = w.astype(BF16)
    return jnp.stack([hi, (w - hi.astype(F32)).astype(BF16)])


def kernel(x, p, norm_mix_pre, norm_mix_post, norm_mlp_pre, norm_mlp_post, w_in, w_in_vmix, mu_shift, mu_shift_vmix, w0, w_decay_up, a0, w_iclr_up, v0, w_vmix_up, w_gate_up, k_k, k_a, r_k, gn_gain, gn_bias, dw_w, dw_b, conv_ln_gain, conv_ln_bias, w_out, w_up, w_down, w_ple, norm_ple, w_ple_gate):
    bsz, seq, d_model = x.shape
    depth = w_in.shape[0]
    d_rwkv = w0.shape[1]
    d_conv = dw_w.shape[2]
    m = bsz * seq
    rkv = 3 * d_rwkv
    low = DECAY_RANK + ICLR_RANK + GATE_RANK
    assert d_rwkv % QW == 0 and seq % CHUNK == 0 and d_conv % LANES == 0

    tm = _tile(m, 512, 8)
    tf = _tile(w_up.shape[2], 1024, LANES)
    tp = _tile(seq, 256, CHUNK)
    tc = _tile(seq, 256, CONV_HALO)

    head = jnp.arange(QW) // HEAD_DIM
    e_ones = (head[:, None] == head[None, :]).astype(BF16)
    row = lambda a: a.reshape(1, -1)

    def small_cols(cols, width):
        return jnp.pad(cols, ((0, 0), (0, width - cols.shape[1])))

    x2 = x.reshape(m, d_model)
    v_first = None
    for i in range(depth):
        wi = w_in[i]
        w_rkv = _quad_columns(wi[:, :d_rwkv], wi[:, d_rwkv:2 * d_rwkv], wi[:, 2 * d_rwkv:rkv]).astype(BF16)
        mu_rkv = _quad_columns(row(mu_shift[i, :d_rwkv]), row(mu_shift[i, d_rwkv:2 * d_rwkv]),
                               row(mu_shift[i, 2 * d_rwkv:rkv]))
        w_glu = wi[:, rkv + low:].astype(BF16)
        d0, d1, d2 = rkv, rkv + DECAY_RANK, rkv + DECAY_RANK + ICLR_RANK
        ws = [small_cols(wi[:, d0:d1], SM_ICLR - SM_DECAY),
              small_cols(wi[:, d1:d2], SM_GATE - SM_ICLR),
              small_cols(wi[:, d2:rkv + low], SM_VMIX - SM_GATE)]
        ms = [small_cols(row(mu_shift[i, d0:d1]), SM_ICLR - SM_DECAY),
              small_cols(row(mu_shift[i, d1:d2]), SM_GATE - SM_ICLR),
              small_cols(row(mu_shift[i, d2:rkv + low]), SM_VMIX - SM_GATE)]
        if i == 0:
            ws.append(jnp.zeros((d_model, SM_WIDTH - SM_VMIX), F32))
            ms.append(jnp.zeros((1, SM_WIDTH - SM_VMIX), F32))
            vmix = None
        else:
            ws.append(small_cols(w_in_vmix[i - 1], SM_WIDTH - SM_VMIX))
            ms.append(small_cols(row(mu_shift_vmix[i - 1]), SM_WIDTH - SM_VMIX))
            vmix = (v_first, row(v0[i - 1]), _up_weight(w_vmix_up[i - 1], SM_WIDTH - SM_VMIX))
        w_small = jnp.concatenate(ws, axis=1).astype(BF16)
        mu_small = jnp.concatenate(ms, axis=1)

        at, rt, bt, kt, v, gl, gate, bonus, z_glu, *rest = _prep(
            x2.reshape(bsz, seq, d_model), row(norm_mix_pre[i]), w_rkv, w_small, w_glu, mu_rkv, mu_small,
            row(w0[i]), _up_weight(w_decay_up[i], SM_ICLR - SM_DECAY),
            row(a0[i]), _up_weight(w_iclr_up[i], SM_GATE - SM_ICLR),
            _up_weight(w_gate_up[i], SM_VMIX - SM_GATE),
            row(k_k[i]), row(k_a[i]), row(r_k[i]), e_ones, vmix, tp)
        if i == 0:
            v_first = rest[0]
        y_r = _wkv(at, rt, bt, kt, v, gl, gate, bonus, row(gn_gain[i]), row(gn_bias[i]), e_ones,
                   _tile(bsz, WKV_ROWS, 1))
        y_c = _conv(z_glu, dw_w[i], row(dw_b[i]), row(conv_ln_gain[i]), row(conv_ln_bias[i]), tc)

        wo = w_out[i].astype(BF16)
        x2 = _outproj(y_r.reshape(m, d_rwkv), y_c.reshape(m, d_conv), x2,
                      wo[:d_rwkv], wo[d_rwkv:], row(norm_mix_post[i]), tm)
        x2 = _mlp(x2, row(norm_mlp_pre[i]), w_up[i].astype(BF16), w_down[i].astype(BF16),
                  row(norm_mlp_post[i]), tm, tf)
        x2 = _ple(x2, p[i].reshape(m, -1), w_ple[i].astype(BF16), row(norm_ple[i]),
                  w_ple_gate[i].astype(BF16), tm)
    return x2.reshape(bsz, seq, d_model)
```

```python
import functools

import jax
import jax.numpy as jnp
from jax import lax
from jax.experimental import pallas as pl
from jax.experimental.pallas import tpu as pltpu

F32 = jnp.float32
BF16 = jnp.bfloat16

HEAD_DIM = 64
DECAY_RANK = 64
ICLR_RANK = 64
VMIX_RANK = 32
GATE_RANK = 160
CONV_WIDTH = 31
RMS_EPS = 1e-6
GN_EPS = 64e-5
LN_EPS = 1e-5

LANES = 128
SUBLANES = 8
CONV_ROWS = 128
QUAD = 4
QW = QUAD * HEAD_DIM
CHUNK = 64
WKV_ROWS = 4
SM_DECAY, SM_ICLR, SM_GATE, SM_VMIX, SM_WIDTH = 0, 128, 256, 512, 640
CONV_HALO = 32
VMEM_LIMIT = 56 * 1024 * 1024


def _cparams(*sem):
    return pltpu.CompilerParams(dimension_semantics=sem, vmem_limit_bytes=VMEM_LIMIT)


def _rms(x, g):
    return x * lax.rsqrt(jnp.mean(x * x, axis=-1, keepdims=True) + RMS_EPS) * g


def _split_dot(x, w_bf16):
    hi = x.astype(BF16)
    lo = (x - hi.astype(F32)).astype(BF16)
    n = x.shape[0]
    both = jnp.dot(jnp.concatenate([hi, lo], axis=0), w_bf16, preferred_element_type=F32)
    return both[:n] + both[n:]


def _shift(z, prev_row, mu):
    rows = lax.broadcasted_iota(jnp.int32, z.shape, 0)
    prev = jnp.where(rows == 0, prev_row, pltpu.roll(z, 1, 0))
    return z + (prev - z) * mu


def _prep_kernel(has_vmix, *refs):
    (x_ref, g_ref, w_ref, ws_ref, wglu_ref, mu_ref, mus_ref,
     w0_ref, wdu_ref, a0_ref, wiu_ref, wgu_ref, kk_ref, ka_ref, rk_ref, e_ref) = refs[:16]
    refs = refs[16:]
    if has_vmix:
        vf_ref, v0_ref, wvu_ref = refs[:3]
        refs = refs[3:]
    at_out, rt_out, bt_out, kt_out, v_out, gl_out, g_out, bo_out, u_out = refs[:9]
    refs = refs[9:]
    if not has_vmix:
        vfirst_out = refs[0]
        refs = refs[1:]
    h_ref, z_ref, carry_ref, carry_s_ref, lw_ref, cum_ref, iclr_ref = refs[:7]
    mix_ref = refs[7] if has_vmix else None

    tp = x_ref.shape[1]
    d = w0_ref.shape[1]
    nq = d // QW

    @pl.when(pl.program_id(1) == 0)
    def _():
        carry_ref[...] = jnp.zeros_like(carry_ref)
        carry_s_ref[...] = jnp.zeros_like(carry_s_ref)

    h_ref[...] = _rms(x_ref[0], g_ref[...]).astype(BF16)

    def project(q, slot):
        z_ref[slot] = jnp.dot(h_ref[...], w_ref[:, q * 3 * QW:(q + 1) * 3 * QW],
                              preferred_element_type=F32)

    project(0, 0)
    zs_raw = jnp.dot(h_ref[...], ws_ref[...], preferred_element_type=F32)
    zs = _shift(zs_raw, carry_s_ref[0:1, :], mus_ref[...])
    carry_s_ref[0:1, :] = zs_raw[tp - 1:tp, :]

    def up(act, w_ref):
        hi = act.astype(BF16)
        lo = (act - hi.astype(F32)).astype(BF16)
        n = act.shape[0]
        both = jnp.dot(jnp.concatenate([hi, lo], axis=0), w_ref[0], preferred_element_type=F32)
        return both[:n] + both[n:] + jnp.dot(hi, w_ref[1], preferred_element_type=F32)

    def up_packed(act, w_ref):
        hi = act.astype(BF16)
        hi32 = hi.astype(F32)
        packed = (hi32 + pltpu.roll(act - hi32, LANES // 2, 1)).astype(BF16)
        return jnp.dot(jnp.concatenate([packed, hi], axis=1), w_ref[...], preferred_element_type=F32)

    w_raw = w0_ref[...] + up_packed(jnp.tanh(zs[:, SM_DECAY:SM_ICLR]), wdu_ref)
    lw = -jnp.exp(-0.5) * jax.nn.sigmoid(w_raw)
    lw_ref[...] = lw
    iclr_ref[...] = jax.nn.sigmoid(a0_ref[...] + up_packed(zs[:, SM_ICLR:SM_GATE], wiu_ref))
    g_out[0] = up(jax.nn.sigmoid(zs[:, SM_GATE:SM_VMIX]), wgu_ref)
    if has_vmix:
        mix_ref[...] = jax.nn.sigmoid(v0_ref[...] + up_packed(zs[:, SM_VMIX:SM_WIDTH], wvu_ref))

    ti = lax.broadcasted_iota(jnp.int32, (tp, tp), 0)
    tj = lax.broadcasted_iota(jnp.int32, (tp, tp), 1)
    tri = ((ti >= tj) & (ti // CHUNK == tj // CHUNK)).astype(BF16)
    hi = lw.astype(BF16)
    rest = lw - hi.astype(F32)
    mid = rest.astype(BF16)
    lo = (rest - mid.astype(F32)).astype(BF16)
    cum = (jnp.dot(tri, hi, preferred_element_type=F32) + jnp.dot(tri, mid, preferred_element_type=F32)
           + jnp.dot(tri, lo, preferred_element_type=F32))
    cum_ref[...] = cum
    for j in range(tp // CHUNK):
        last = (j + 1) * CHUNK - 1
        gl_out[0, j] = jnp.exp(cum[last:last + 1, :])

    e = e_ref[...]
    gw = wglu_ref.shape[1] // (2 * nq)
    for q in range(nq):
        if q + 1 < nq:
            project(q + 1, (q + 1) % 2)
        zg = jnp.dot(h_ref[...], wglu_ref[:, q * 2 * gw:(q + 1) * 2 * gw],
                     preferred_element_type=F32)
        u_out[0, :, q * gw:(q + 1) * gw] = zg[:, :gw] * jax.nn.sigmoid(zg[:, gw:])
        qc = slice(q * QW, (q + 1) * QW)
        zc = slice(q * 3 * QW, (q + 1) * 3 * QW)
        z = z_ref[q % 2]
        zq = _shift(z, carry_ref[0:1, zc], mu_ref[:, zc])
        carry_ref[0:1, zc] = z[tp - 1:tp, :]
        r, k, v = zq[:, :QW], zq[:, QW:2 * QW], zq[:, 2 * QW:]
        iclr, lw, cum = iclr_ref[:, qc], lw_ref[:, qc], cum_ref[:, qc]
        if has_vmix:
            v = v + (vf_ref[0, :, qc] - v) * mix_ref[:, qc]
        else:
            vfirst_out[0, :, qc] = v
        v_out[0, :, qc] = v.astype(BF16)
        kk = k * kk_ref[:, qc]
        kk = kk / jnp.maximum(jnp.sqrt(_split_dot(kk * kk, e)), 1e-12)
        k = k * (1.0 + (iclr - 1.0) * ka_ref[:, qc])
        bo_out[0, :, qc] = _split_dot(r * k * rk_ref[:, qc], e) * v
        g_inv = jnp.exp(-cum)
        at_out[0, :, qc] = (-kk * jnp.exp(cum - lw)).astype(BF16)
        rt_out[0, :, qc] = (r * jnp.exp(cum)).astype(BF16)
        bt_out[0, :, qc] = (kk * iclr * g_inv).astype(BF16)
        kt_out[0, :, qc] = (k * g_inv).astype(BF16)


def _quad_columns(r, k, v):
    lead, d = r.shape[:-1], r.shape[-1]
    parts = [t.reshape(*lead, d // QW, QW) for t in (r, k, v)]
    return jnp.stack(parts, axis=-2).reshape(*lead, 3 * d)


def _prep(x3, g, w_rkv, w_small, w_glu, mu, mus, w0, wdu, a0, wiu, wgu, k_k, k_a, r_k, e_ones,
          vmix, tp):
    b, s, dm = x3.shape
    d = w0.shape[1]
    dg = w_glu.shape[1]
    has_vmix = vmix is not None
    row = lambda bi, t: (bi, t, 0)

    def const(shape):
        return pl.BlockSpec(shape, lambda bi, t: (0,) * len(shape))

    in_specs = [pl.BlockSpec((1, tp, dm), row), const((1, dm)), const((dm, 3 * d)),
                const((dm, SM_WIDTH)), const((dm, dg)), const((1, 3 * d)), const((1, SM_WIDTH)),
                const((1, d)), const((2 * LANES, d)), const((1, d)), const((2 * LANES, d)),
                const((2, SM_VMIX - SM_GATE, d)), const((1, d)), const((1, d)), const((1, d)),
                const((QW, QW))]
    args = [x3, g, w_rkv, w_small, w_glu, mu, mus, w0, wdu, a0, wiu, wgu, k_k, k_a, r_k, e_ones]
    wide = pltpu.VMEM((tp, d), F32)
    scratch = [pltpu.VMEM((tp, dm), BF16), pltpu.VMEM((2, tp, 3 * QW), F32),
               pltpu.VMEM((SUBLANES, 3 * d), F32), pltpu.VMEM((SUBLANES, SM_WIDTH), F32),
               wide, wide, wide]
    if has_vmix:
        scratch.append(wide)
        v_first, v0, wvu = vmix
        in_specs += [pl.BlockSpec((1, tp, d), row), const((1, d)), const((2 * LANES, d))]
        args += [v_first, v0, wvu]
    tile = pl.BlockSpec((1, tp, d), row)
    nch = tp // CHUNK
    out_specs = ([tile] * 5 + [pl.BlockSpec((1, nch, 1, d), lambda bi, t: (bi, t, 0, 0)), tile, tile]
                 + [pl.BlockSpec((1, tp, dg // 2), row)])
    out_shape = ([jax.ShapeDtypeStruct((b, s, d), BF16)] * 5
                 + [jax.ShapeDtypeStruct((b, s // CHUNK, 1, d), F32)]
                 + [jax.ShapeDtypeStruct((b, s, d), F32)] * 2
                 + [jax.ShapeDtypeStruct((b, s, dg // 2), F32)])
    if not has_vmix:
        out_specs.append(tile)
        out_shape.append(jax.ShapeDtypeStruct((b, s, d), F32))
    return pl.pallas_call(
        functools.partial(_prep_kernel, has_vmix),
        grid=(b, s // tp),
        in_specs=in_specs,
        out_specs=out_specs,
        out_shape=out_shape,
        scratch_shapes=scratch,
        compiler_params=_cparams("parallel", "arbitrary"),
        name="rwkv_prep",
    )(*args)


def _stack(x, block_diag):
    return jnp.where(block_diag, jnp.concatenate([x] * QUAD, axis=0), jnp.zeros((), x.dtype))


def _nt(a, b):
    return lax.dot_general(a, b, (((1,), (1,)), ((), ())), preferred_element_type=F32)


def _tn(a, b):
    return lax.dot_general(a, b, (((0,), (0,)), ((), ())), preferred_element_type=F32)


def _mm(a, b):
    return jnp.dot(a, b, preferred_element_type=F32)


def _wkv_kernel(at_ref, rt_ref, bt_ref, kt_ref, v_ref, gl_ref, g_ref, bo_ref, gg_ref, gb_ref,
                e_ref, y_ref, s_ref):
    @pl.when(pl.program_id(1) == 0)
    def _():
        s_ref[...] = jnp.zeros_like(s_ref)

    nb, c, d = at_ref.shape
    nq = d // QW
    quads = range(nb * nq)
    ri = lax.broadcasted_iota(jnp.int32, (QUAD * c, QW), 0)
    li = lax.broadcasted_iota(jnp.int32, (QUAD * c, QW), 1)
    ti = lax.broadcasted_iota(jnp.int32, (c, QW), 0)
    si = lax.broadcasted_iota(jnp.int32, (c, QW), 1) % c
    block_diag, strict, incl, eye = ri // c == li // HEAD_DIM, ti > si, ti >= si, ti == si
    bf = lambda t: t.astype(BF16)
    stack = lambda t: _stack(t, block_diag)
    cat = lambda *t: jnp.concatenate(t, axis=0)
    lanes = lambda ref, q: ref[q // nq, :, (q % nq) * QW:(q % nq + 1) * QW]

    bt = [lanes(bt_ref, q) for q in quads]
    kt = [lanes(kt_ref, q) for q in quads]
    vv = [lanes(v_ref, q) for q in quads]
    lhs = [cat(lanes(at_ref, q), lanes(rt_ref, q)) for q in quads]
    gram = [_nt(lhs[q], cat(stack(bt[q]), stack(kt[q]))) for q in quads]
    l_ab = [jnp.where(strict, g[:c, :QW], 0.0) for g in gram]
    tri_k = [bf(cat(jnp.where(strict, g[:c, QW:], 0.0), jnp.where(incl, g[c:, QW:], 0.0)))
             for g in gram]
    p_rb = [bf(jnp.where(incl, g[c:, :QW], 0.0)) for g in gram]

    state = [s_ref[q] for q in quads]
    from_state = [_nt(lhs[q], bf(state[q])) for q in quads]
    from_v = [_mm(tri_k[q], stack(vv[q])) for q in quads]

    t_inv = [jnp.where(eye, 1.0, l) for l in l_ab]
    power = [bf(l) for l in l_ab]
    power = [bf(_mm(p, stack(p))) for p in power]
    for _ in range(c.bit_length() - 3):
        both = [_mm(cat(power[q], bf(t_inv[q])), stack(power[q])) for q in quads]
        power = [bf(b[:c]) for b in both]
        t_inv = [t_inv[q] + both[q][c:] for q in quads]
    t_inv = [t_inv[q] + _mm(bf(t_inv[q]), stack(power[q])) for q in quads]

    u = [bf(_mm(bf(t_inv[q]), stack(bf(from_state[q][:c] + from_v[q][:c])))) for q in quads]
    y = [from_state[q][c:] + from_v[q][c:] + _mm(p_rb[q], stack(u[q])) for q in quads]
    upd = [_tn(cat(u[q], vv[q]), cat(bt[q], kt[q])) for q in quads]
    for q in quads:
        g_last = gl_ref[q // nq, 0, :, (q % nq) * QW:(q % nq + 1) * QW]
        s_ref[q] = (state[q] + jnp.where(block_diag, upd[q], 0.0)) * g_last

    e = e_ref[...]
    inv_n = 1.0 / HEAD_DIM
    mean = [_split_dot(t, e) * inv_n for t in y]
    yc = [y[q] - mean[q] for q in quads]
    var = [_split_dot(t * t, e) * inv_n for t in yc]
    for q in quads:
        bi, sl = q // nq, slice((q % nq) * QW, (q % nq + 1) * QW)
        out = (yc[q] * lax.rsqrt(var[q] + GN_EPS) * gg_ref[:, sl] + gb_ref[:, sl]
               + bo_ref[bi, :, sl]) * g_ref[bi, :, sl]
        y_ref[bi, :, sl] = out.astype(y_ref.dtype)


def _wkv(at, rt, bt, kt, v, gl, gate, bonus, gn_gain, gn_bias, e_ones, nb):
    assert CHUNK == HEAD_DIM
    b, s, d = at.shape
    blk = pl.BlockSpec((nb, CHUNK, d), lambda bi, c: (bi, c, 0))
    vec = pl.BlockSpec((1, d), lambda bi, c: (0, 0))
    return pl.pallas_call(
        _wkv_kernel,
        grid=(b // nb, s // CHUNK),
        in_specs=([blk] * 5 + [pl.BlockSpec((nb, 1, 1, d), lambda bi, c: (bi, c, 0, 0))]
                  + [blk] * 2 + [vec] * 2 + [pl.BlockSpec((QW, QW), lambda bi, c: (0, 0))]),
        out_specs=blk,
        out_shape=jax.ShapeDtypeStruct((b, s, d), BF16),
        scratch_shapes=[pltpu.VMEM((nb * d // QW, QW, QW), F32)],
        compiler_params=_cparams("parallel", "arbitrary"),
        name="wkv",
    )(at, rt, bt, kt, v, gl, gate, bonus, gn_gain, gn_bias, e_ones)


def _conv_kernel(g_ref, w_ref, b_ref, lg_ref, lb_ref, y_ref, u_ref, c_ref, sh_ref):
    tc = g_ref.shape[1]
    d = y_ref.shape[2]

    @pl.when(pl.program_id(1) == 0)
    def _():
        u_ref[0:CONV_HALO, :] = jnp.zeros((CONV_HALO, d), F32)

    u_ref[CONV_HALO:CONV_HALO + tc, :] = g_ref[0]

    base = CONV_HALO - (CONV_WIDTH - 1)
    rb = min(tc, CONV_ROWS)
    nt = rb // SUBLANES

    def strip(s, carry):
        cols = pl.ds(pl.multiple_of(s * LANES, LANES), LANES)
        bias = jnp.broadcast_to(b_ref[:, cols], (SUBLANES, LANES))
        for r0 in range(0, tc, rb):
            acc = [bias] * nt
            for res in range(SUBLANES):
                offs = [o for o in range(base, base + CONV_WIDTH) if o % SUBLANES == res]
                lo, hi = offs[0], offs[-1]
                n = hi - lo + rb
                if res:
                    sh_ref[res, 0:n, :] = u_ref[r0 + lo:r0 + lo + n, cols]
                    src, at = sh_ref.at[res], 0
                else:
                    src, at = u_ref.at[:, cols], r0 + lo
                win = [src[at + i * SUBLANES:at + (i + 1) * SUBLANES, :] for i in range(n // SUBLANES)]
                for o in offs:
                    tap = jnp.broadcast_to(w_ref[o - base:o - base + 1, cols], (SUBLANES, LANES))
                    k = (o - lo) // SUBLANES
                    acc = [acc[i] + win[i + k] * tap for i in range(nt)]
            for i in range(nt):
                c_ref[r0 + i * SUBLANES:r0 + (i + 1) * SUBLANES, cols] = acc[i]
        return carry

    lax.fori_loop(0, d // LANES, strip, 0)
    acc = c_ref[...]
    mean = jnp.mean(acc, axis=-1, keepdims=True)
    cen = acc - mean
    var = jnp.mean(cen * cen, axis=-1, keepdims=True)
    cf = cen * lax.rsqrt(var + LN_EPS) * lg_ref[...] + lb_ref[...]
    y_ref[0] = (cf * jax.nn.sigmoid(cf)).astype(y_ref.dtype)
    u_ref[0:CONV_HALO, :] = u_ref[tc:tc + CONV_HALO, :]


def _conv(u_glu, dw_w, dw_b, ln_g, ln_b, tc):
    b, s, d = u_glu.shape
    const = lambda shape: pl.BlockSpec(shape, lambda bi, t: (0, 0))
    return pl.pallas_call(
        _conv_kernel,
        grid=(b, s // tc),
        in_specs=[pl.BlockSpec((1, tc, d), lambda bi, t: (bi, t, 0)),
                  const((CONV_WIDTH, d)), const((1, d)), const((1, d)), const((1, d))],
        out_specs=pl.BlockSpec((1, tc, d), lambda bi, t: (bi, t, 0)),
        out_shape=jax.ShapeDtypeStruct((b, s, d), BF16),
        scratch_shapes=[pltpu.VMEM((tc + CONV_HALO, d), F32), pltpu.VMEM((tc, d), F32),
                        pltpu.VMEM((SUBLANES, min(tc, CONV_ROWS) + CONV_HALO, LANES), F32)],
        compiler_params=_cparams("parallel", "arbitrary"),
        name="conformer_conv",
    )(u_glu, dw_w, dw_b, ln_g, ln_b)


def _outproj_kernel(yr_ref, yc_ref, x_ref, wr_ref, wc_ref, g_ref, o_ref):
    mixed = (jnp.dot(yr_ref[...], wr_ref[...], preferred_element_type=F32)
             + jnp.dot(yc_ref[...], wc_ref[...], preferred_element_type=F32))
    o_ref[...] = x_ref[...] + _rms(mixed, g_ref[...])


def _outproj(y_r, y_c, x2, w_r, w_c, g, tm):
    m, d = x2.shape
    dr, dc = y_r.shape[1], y_c.shape[1]
    return pl.pallas_call(
        _outproj_kernel,
        grid=(m // tm,),
        in_specs=[pl.BlockSpec((tm, dr), lambda i: (i, 0)),
                  pl.BlockSpec((tm, dc), lambda i: (i, 0)),
                  pl.BlockSpec((tm, d), lambda i: (i, 0)),
                  pl.BlockSpec((dr, d), lambda i: (0, 0)),
                  pl.BlockSpec((dc, d), lambda i: (0, 0)),
                  pl.BlockSpec((1, d), lambda i: (0, 0))],
        out_specs=pl.BlockSpec((tm, d), lambda i: (i, 0)),
        out_shape=jax.ShapeDtypeStruct((m, d), F32),
        compiler_params=_cparams("parallel"),
        name="outproj",
    )(y_r, y_c, x2, w_r, w_c, g)


def _mlp_kernel(x_ref, g1_ref, wu_ref, wd_ref, g2_ref, o_ref, h_ref, acc_ref):
    j = pl.program_id(1)

    @pl.when(j == 0)
    def _():
        h_ref[...] = _rms(x_ref[...], g1_ref[...]).astype(BF16)
        acc_ref[...] = jnp.zeros_like(acc_ref)

    a = jnp.maximum(jnp.dot(h_ref[...], wu_ref[...], preferred_element_type=F32), 0.0)
    acc_ref[...] += jnp.dot((a * a).astype(BF16), wd_ref[...], preferred_element_type=F32)

    @pl.when(j == pl.num_programs(1) - 1)
    def _():
        o_ref[...] = x_ref[...] + _rms(acc_ref[...], g2_ref[...])


def _mlp(x2, g1, w_up, w_down, g2, tm, tf):
    m, d = x2.shape
    f = w_up.shape[1]
    return pl.pallas_call(
        _mlp_kernel,
        grid=(m // tm, f // tf),
        in_specs=[pl.BlockSpec((tm, d), lambda i, j: (i, 0)),
                  pl.BlockSpec((1, d), lambda i, j: (0, 0)),
                  pl.BlockSpec((d, tf), lambda i, j: (0, j)),
                  pl.BlockSpec((tf, d), lambda i, j: (j, 0)),
                  pl.BlockSpec((1, d), lambda i, j: (0, 0))],
        out_specs=pl.BlockSpec((tm, d), lambda i, j: (i, 0)),
        out_shape=jax.ShapeDtypeStruct((m, d), F32),
        scratch_shapes=[pltpu.VMEM((tm, d), BF16), pltpu.VMEM((tm, d), F32)],
        compiler_params=_cparams("parallel", "arbitrary"),
        name="mlp",
    )(x2, g1, w_up, w_down, g2)


def _ple_kernel(x_ref, p_ref, wp_ref, g_ref, wg_ref, o_ref):
    x = x_ref[...]
    e = _rms(jnp.dot(p_ref[...].astype(BF16), wp_ref[...], preferred_element_type=F32), g_ref[...])
    gate = jax.nn.sigmoid(jnp.dot(x.astype(BF16), wg_ref[...], preferred_element_type=F32))
    o_ref[...] = x + gate * e


def _ple(x2, p2, w_ple, g, w_gate, tm):
    m, d = x2.shape
    dp = p2.shape[1]
    return pl.pallas_call(
        _ple_kernel,
        grid=(m // tm,),
        in_specs=[pl.BlockSpec((tm, d), lambda i: (i, 0)),
                  pl.BlockSpec((tm, dp), lambda i: (i, 0)),
                  pl.BlockSpec((dp, d), lambda i: (0, 0)),
                  pl.BlockSpec((1, d), lambda i: (0, 0)),
                  pl.BlockSpec((d, d), lambda i: (0, 0))],
        out_specs=pl.BlockSpec((tm, d), lambda i: (i, 0)),
        out_shape=jax.ShapeDtypeStruct((m, d), F32),
        compiler_params=_cparams("parallel"),
        name="ple",
    )(x2, p2, w_ple, g, w_gate)


def _tile(n, want, align):
    t = (min(n, want) // align) * align
    while n % t:
        t -= align
    return t


def _up_weight(w, rows):
    w = jnp.pad(w, ((0, rows - w.shape[0]), (0, 0)))
    hi = w.astype(BF16)
    return jnp.stack([hi, (w - hi.astype(F32)).astype(BF16)])


def _up_weight_packed(w):
    half = LANES // 2
    assert w.shape[0] <= half
    w = jnp.pad(w, ((0, half - w.shape[0]), (0, 0)))
    hi = w.astype(BF16)
    lo = (w - hi.astype(F32)).astype(BF16)
    return jnp.concatenate([hi, hi, lo, jnp.zeros_like(hi)], axis=0)


def kernel(x, p, norm_mix_pre, norm_mix_post, norm_mlp_pre, norm_mlp_post, w_in, w_in_vmix, mu_shift, mu_shift_vmix, w0, w_decay_up, a0, w_iclr_up, v0, w_vmix_up, w_gate_up, k_k, k_a, r_k, gn_gain, gn_bias, dw_w, dw_b, conv_ln_gain, conv_ln_bias, w_out, w_up, w_down, w_ple, norm_ple, w_ple_gate):
    bsz, seq, d_model = x.shape
    depth = w_in.shape[0]
    d_rwkv = w0.shape[1]
    d_conv = dw_w.shape[2]
    m = bsz * seq
    rkv = 3 * d_rwkv
    low = DECAY_RANK + ICLR_RANK + GATE_RANK
    assert d_rwkv % QW == 0 and seq % CHUNK == 0 and d_conv % LANES == 0

    tm = _tile(m, 512, 8)
    tf = _tile(w_up.shape[2], 1024, LANES)
    tp = _tile(seq, 256, CHUNK)
    tc = _tile(seq, 256, CONV_HALO)

    head = jnp.arange(QW) // HEAD_DIM
    e_ones = (head[:, None] == head[None, :]).astype(BF16)
    row = lambda a: a.reshape(1, -1)

    def small_cols(cols, width):
        return jnp.pad(cols, ((0, 0), (0, width - cols.shape[1])))

    x2 = x.reshape(m, d_model)
    v_first = None
    for i in range(depth):
        wi = w_in[i]
        w_rkv = _quad_columns(wi[:, :d_rwkv], wi[:, d_rwkv:2 * d_rwkv], wi[:, 2 * d_rwkv:rkv]).astype(BF16)
        mu_rkv = _quad_columns(row(mu_shift[i, :d_rwkv]), row(mu_shift[i, d_rwkv:2 * d_rwkv]),
                               row(mu_shift[i, 2 * d_rwkv:rkv]))
        nq = d_rwkv // QW
        w_glu = jnp.stack([wi[:, rkv + low:rkv + low + d_conv].reshape(d_model, nq, d_conv // nq),
                           wi[:, rkv + low + d_conv:].reshape(d_model, nq, d_conv // nq)],
                          axis=2).reshape(d_model, 2 * d_conv).astype(BF16)
        d0, d1, d2 = rkv, rkv + DECAY_RANK, rkv + DECAY_RANK + ICLR_RANK
        ws = [small_cols(wi[:, d0:d1], SM_ICLR - SM_DECAY),
              small_cols(wi[:, d1:d2], SM_GATE - SM_ICLR),
              small_cols(wi[:, d2:rkv + low], SM_VMIX - SM_GATE)]
        ms = [small_cols(row(mu_shift[i, d0:d1]), SM_ICLR - SM_DECAY),
              small_cols(row(mu_shift[i, d1:d2]), SM_GATE - SM_ICLR),
              small_cols(row(mu_shift[i, d2:rkv + low]), SM_VMIX - SM_GATE)]
        if i == 0:
            ws.append(jnp.zeros((d_model, SM_WIDTH - SM_VMIX), F32))
            ms.append(jnp.zeros((1, SM_WIDTH - SM_VMIX), F32))
            vmix = None
        else:
            ws.append(small_cols(w_in_vmix[i - 1], SM_WIDTH - SM_VMIX))
            ms.append(small_cols(row(mu_shift_vmix[i - 1]), SM_WIDTH - SM_VMIX))
            vmix = (v_first, row(v0[i - 1]), _up_weight_packed(w_vmix_up[i - 1]))
        w_small = jnp.concatenate(ws, axis=1).astype(BF16)
        mu_small = jnp.concatenate(ms, axis=1)

        at, rt, bt, kt, v, gl, gate, bonus, u_glu, *rest = _prep(
            x2.reshape(bsz, seq, d_model), row(norm_mix_pre[i]), w_rkv, w_small, w_glu, mu_rkv, mu_small,
            row(w0[i]), _up_weight_packed(w_decay_up[i]),
            row(a0[i]), _up_weight_packed(w_iclr_up[i]),
            _up_weight(w_gate_up[i], SM_VMIX - SM_GATE),
            row(k_k[i]), row(k_a[i]), row(r_k[i]), e_ones, vmix, tp)
        if i == 0:
            v_first = rest[0]
        y_r = _wkv(at, rt, bt, kt, v, gl, gate, bonus, row(gn_gain[i]), row(gn_bias[i]), e_ones,
                   _tile(bsz, WKV_ROWS, 1))
        y_c = _conv(u_glu, dw_w[i], row(dw_b[i]), row(conv_ln_gain[i]), row(conv_ln_bias[i]), tc)

        wo = w_out[i].astype(BF16)
        x2 = _outproj(y_r.reshape(m, d_rwkv), y_c.reshape(m, d_conv), x2,
                      wo[:d_rwkv], wo[d_rwkv:], row(norm_mix_post[i]), tm)
        x2 = _mlp(x2, row(norm_mlp_pre[i]), w_up[i].astype(BF16), w_down[i].astype(BF16),
                  row(norm_mlp_post[i]), tm, tf)
        x2 = _ple(x2, p[i].reshape(m, -1), w_ple[i].astype(BF16), row(norm_ple[i]),
                  w_ple_gate[i].astype(BF16), tm)
    return x2.reshape(bsz, seq, d_model)
```

```python
import functools

import jax
import jax.numpy as jnp
from jax import lax
from jax.experimental import pallas as pl
from jax.experimental.pallas import tpu as pltpu

F32 = jnp.float32
BF16 = jnp.bfloat16

HEAD_DIM = 64
DECAY_RANK = 64
ICLR_RANK = 64
VMIX_RANK = 32
GATE_RANK = 160
CONV_WIDTH = 31
RMS_EPS = 1e-6
GN_EPS = 64e-5
LN_EPS = 1e-5

LANES = 128
SUBLANES = 8
CONV_ROWS = 128
QUAD = 4
QW = QUAD * HEAD_DIM
CHUNK = 64
WKV_ROWS = 4
SM_DECAY, SM_ICLR, SM_GATE, SM_VMIX, SM_WIDTH = 0, 128, 256, 512, 640
CONV_HALO = 32
VMEM_LIMIT = 56 * 1024 * 1024


def _cparams(*sem):
    return pltpu.CompilerParams(dimension_semantics=sem, vmem_limit_bytes=VMEM_LIMIT)


def _rms(x, g):
    return x * lax.rsqrt(jnp.mean(x * x, axis=-1, keepdims=True) + RMS_EPS) * g


def _split_dot(x, w_bf16):
    hi = x.astype(BF16)
    lo = (x - hi.astype(F32)).astype(BF16)
    n = x.shape[0]
    both = jnp.dot(jnp.concatenate([hi, lo], axis=0), w_bf16, preferred_element_type=F32)
    return both[:n] + both[n:]


def _shift(z, prev_row, mu):
    rows = lax.broadcasted_iota(jnp.int32, z.shape, 0)
    prev = jnp.where(rows == 0, prev_row, pltpu.roll(z, 1, 0))
    return z + (prev - z) * mu


def _prep_kernel(has_vmix, *refs):
    (x_ref, g_ref, w_ref, ws_ref, wglu_ref, mu_ref, mus_ref,
     w0_ref, wdu_ref, a0_ref, wiu_ref, wgu_ref, kk_ref, ka_ref, rk_ref, e_ref) = refs[:16]
    refs = refs[16:]
    if has_vmix:
        vf_ref, v0_ref, wvu_ref = refs[:3]
        refs = refs[3:]
    at_out, rt_out, bt_out, kt_out, v_out, gl_out, g_out, bo_out, u_out = refs[:9]
    refs = refs[9:]
    if not has_vmix:
        vfirst_out = refs[0]
        refs = refs[1:]
    h_ref, z_ref, carry_ref, carry_s_ref, lw_ref, cum_ref, iclr_ref = refs[:7]
    mix_ref = refs[7] if has_vmix else None

    tp = x_ref.shape[1]
    d = w0_ref.shape[1]
    nq = d // QW

    @pl.when(pl.program_id(1) == 0)
    def _():
        carry_ref[...] = jnp.zeros_like(carry_ref)
        carry_s_ref[...] = jnp.zeros_like(carry_s_ref)

    h_ref[...] = _rms(x_ref[0], g_ref[...]).astype(BF16)

    def project(q, slot):
        z_ref[slot] = jnp.dot(h_ref[...], w_ref[:, q * 3 * QW:(q + 1) * 3 * QW],
                              preferred_element_type=F32)

    project(0, 0)
    zs_raw = jnp.dot(h_ref[...], ws_ref[...], preferred_element_type=F32)
    zs = _shift(zs_raw, carry_s_ref[0:1, :], mus_ref[...])
    carry_s_ref[0:1, :] = zs_raw[tp - 1:tp, :]

    def up(act, w_ref):
        hi = act.astype(BF16)
        lo = (act - hi.astype(F32)).astype(BF16)
        n = act.shape[0]
        both = jnp.dot(jnp.concatenate([hi, lo], axis=0), w_ref[0], preferred_element_type=F32)
        return both[:n] + both[n:] + jnp.dot(hi, w_ref[1], preferred_element_type=F32)

    def up_packed(act, w_ref):
        hi = act.astype(BF16)
        hi32 = hi.astype(F32)
        packed = (hi32 + pltpu.roll(act - hi32, LANES // 2, 1)).astype(BF16)
        return jnp.dot(jnp.concatenate([packed, hi], axis=1), w_ref[...], preferred_element_type=F32)

    w_raw = w0_ref[...] + up_packed(jnp.tanh(zs[:, SM_DECAY:SM_ICLR]), wdu_ref)
    lw = -jnp.exp(-0.5) * jax.nn.sigmoid(w_raw)
    lw_ref[...] = lw
    iclr_ref[...] = jax.nn.sigmoid(a0_ref[...] + up_packed(zs[:, SM_ICLR:SM_GATE], wiu_ref))
    g_out[0] = up(jax.nn.sigmoid(zs[:, SM_GATE:SM_VMIX]), wgu_ref)
    if has_vmix:
        mix_ref[...] = jax.nn.sigmoid(v0_ref[...] + up_packed(zs[:, SM_VMIX:SM_WIDTH], wvu_ref))

    ti = lax.broadcasted_iota(jnp.int32, (tp, tp), 0)
    tj = lax.broadcasted_iota(jnp.int32, (tp, tp), 1)
    tri = ((ti >= tj) & (ti // CHUNK == tj // CHUNK)).astype(BF16)
    hi = lw.astype(BF16)
    rest = lw - hi.astype(F32)
    mid = rest.astype(BF16)
    lo = (rest - mid.astype(F32)).astype(BF16)
    cum = (jnp.dot(tri, hi, preferred_element_type=F32) + jnp.dot(tri, mid, preferred_element_type=F32)
           + jnp.dot(tri, lo, preferred_element_type=F32))
    cum_ref[...] = cum
    for j in range(tp // CHUNK):
        last = (j + 1) * CHUNK - 1
        gl_out[0, j] = jnp.exp(cum[last:last + 1, :])

    e = e_ref[...]
    gw = wglu_ref.shape[1] // (2 * nq)
    for q in range(nq):
        if q + 1 < nq:
            project(q + 1, (q + 1) % 2)
        lin = jnp.dot(h_ref[...], wglu_ref[:, q * gw:(q + 1) * gw], preferred_element_type=F32)
        gate = jnp.dot(h_ref[...], wglu_ref[:, (nq + q) * gw:(nq + q + 1) * gw],
                       preferred_element_type=F32)
        u_out[0, :, q * gw:(q + 1) * gw] = lin * jax.nn.sigmoid(gate)
        qc = slice(q * QW, (q + 1) * QW)
        zc = slice(q * 3 * QW, (q + 1) * 3 * QW)
        z = z_ref[q % 2]
        zq = _shift(z, carry_ref[0:1, zc], mu_ref[:, zc])
        carry_ref[0:1, zc] = z[tp - 1:tp, :]
        r, k, v = zq[:, :QW], zq[:, QW:2 * QW], zq[:, 2 * QW:]
        iclr, lw, cum = iclr_ref[:, qc], lw_ref[:, qc], cum_ref[:, qc]
        if has_vmix:
            v = v + (vf_ref[0, :, qc] - v) * mix_ref[:, qc]
        else:
            vfirst_out[0, :, qc] = v
        v_out[0, :, qc] = v.astype(BF16)
        kk = k * kk_ref[:, qc]
        kk = kk / jnp.maximum(jnp.sqrt(_split_dot(kk * kk, e)), 1e-12)
        k = k * (1.0 + (iclr - 1.0) * ka_ref[:, qc])
        bo_out[0, :, qc] = _split_dot(r * k * rk_ref[:, qc], e) * v
        g_inv = jnp.exp(-cum)
        at_out[0, :, qc] = (-kk * jnp.exp(cum - lw)).astype(BF16)
        rt_out[0, :, qc] = (r * jnp.exp(cum)).astype(BF16)
        bt_out[0, :, qc] = (kk * iclr * g_inv).astype(BF16)
        kt_out[0, :, qc] = (k * g_inv).astype(BF16)


def _quad_columns(r, k, v):
    lead, d = r.shape[:-1], r.shape[-1]
    parts = [t.reshape(*lead, d // QW, QW) for t in (r, k, v)]
    return jnp.stack(parts, axis=-2).reshape(*lead, 3 * d)


def _prep(x3, g, w_rkv, w_small, w_glu, mu, mus, w0, wdu, a0, wiu, wgu, k_k, k_a, r_k, e_ones,
          vmix, tp):
    b, s, dm = x3.shape
    d = w0.shape[1]
    dg = w_glu.shape[1]
    has_vmix = vmix is not None
    row = lambda bi, t: (bi, t, 0)

    def const(shape):
        return pl.BlockSpec(shape, lambda bi, t: (0,) * len(shape))

    in_specs = [pl.BlockSpec((1, tp, dm), row), const((1, dm)), const((dm, 3 * d)),
                const((dm, SM_WIDTH)), const((dm, dg)), const((1, 3 * d)), const((1, SM_WIDTH)),
                const((1, d)), const((2 * LANES, d)), const((1, d)), const((2 * LANES, d)),
                const((2, SM_VMIX - SM_GATE, d)), const((1, d)), const((1, d)), const((1, d)),
                const((QW, QW))]
    args = [x3, g, w_rkv, w_small, w_glu, mu, mus, w0, wdu, a0, wiu, wgu, k_k, k_a, r_k, e_ones]
    wide = pltpu.VMEM((tp, d), F32)
    scratch = [pltpu.VMEM((tp, dm), BF16), pltpu.VMEM((2, tp, 3 * QW), F32),
               pltpu.VMEM((SUBLANES, 3 * d), F32), pltpu.VMEM((SUBLANES, SM_WIDTH), F32),
               wide, wide, wide]
    if has_vmix:
        scratch.append(wide)
        v_first, v0, wvu = vmix
        in_specs += [pl.BlockSpec((1, tp, d), row), const((1, d)), const((2 * LANES, d))]
        args += [v_first, v0, wvu]
    tile = pl.BlockSpec((1, tp, d), row)
    nch = tp // CHUNK
    out_specs = ([tile] * 5 + [pl.BlockSpec((1, nch, 1, d), lambda bi, t: (bi, t, 0, 0)), tile, tile]
                 + [pl.BlockSpec((1, tp, dg // 2), row)])
    out_shape = ([jax.ShapeDtypeStruct((b, s, d), BF16)] * 5
                 + [jax.ShapeDtypeStruct((b, s // CHUNK, 1, d), F32)]
                 + [jax.ShapeDtypeStruct((b, s, d), F32)] * 2
                 + [jax.ShapeDtypeStruct((b, s, dg // 2), F32)])
    if not has_vmix:
        out_specs.append(tile)
        out_shape.append(jax.ShapeDtypeStruct((b, s, d), F32))
    return pl.pallas_call(
        functools.partial(_prep_kernel, has_vmix),
        grid=(b, s // tp),
        in_specs=in_specs,
        out_specs=out_specs,
        out_shape=out_shape,
        scratch_shapes=scratch,
        compiler_params=_cparams("parallel", "arbitrary"),
        name="rwkv_prep",
    )(*args)


def _stack(x, block_diag):
    return jnp.where(block_diag, jnp.concatenate([x] * QUAD, axis=0), jnp.zeros((), x.dtype))


def _nt(a, b):
    return lax.dot_general(a, b, (((1,), (1,)), ((), ())), preferred_element_type=F32)


def _tn(a, b):
    return lax.dot_general(a, b, (((0,), (0,)), ((), ())), preferred_element_type=F32)


def _mm(a, b):
    return jnp.dot(a, b, preferred_element_type=F32)


def _wkv_kernel(at_ref, rt_ref, bt_ref, kt_ref, v_ref, gl_ref, g_ref, bo_ref, gg_ref, gb_ref,
                e_ref, y_ref, s_ref):
    @pl.when(pl.program_id(1) == 0)
    def _():
        s_ref[...] = jnp.zeros_like(s_ref)

    nb, c, d = at_ref.shape
    nq = d // QW
    quads = range(nb * nq)
    ri = lax.broadcasted_iota(jnp.int32, (QUAD * c, QW), 0)
    li = lax.broadcasted_iota(jnp.int32, (QUAD * c, QW), 1)
    ti = lax.broadcasted_iota(jnp.int32, (c, QW), 0)
    si = lax.broadcasted_iota(jnp.int32, (c, QW), 1) % c
    block_diag, strict, incl, eye = ri // c == li // HEAD_DIM, ti > si, ti >= si, ti == si
    bf = lambda t: t.astype(BF16)
    stack = lambda t: _stack(t, block_diag)
    cat = lambda *t: jnp.concatenate(t, axis=0)
    lanes = lambda ref, q: ref[q // nq, :, (q % nq) * QW:(q % nq + 1) * QW]

    bt = [lanes(bt_ref, q) for q in quads]
    kt = [lanes(kt_ref, q) for q in quads]
    vv = [lanes(v_ref, q) for q in quads]
    lhs = [cat(lanes(at_ref, q), lanes(rt_ref, q)) for q in quads]
    gram = [_nt(lhs[q], cat(stack(bt[q]), stack(kt[q]))) for q in quads]
    l_ab = [jnp.where(strict, g[:c, :QW], 0.0) for g in gram]
    tri_k = [bf(cat(jnp.where(strict, g[:c, QW:], 0.0), jnp.where(incl, g[c:, QW:], 0.0)))
             for g in gram]
    p_rb = [bf(jnp.where(incl, g[c:, :QW], 0.0)) for g in gram]

    state = [s_ref[q] for q in quads]
    from_state = [_nt(lhs[q], bf(state[q])) for q in quads]
    from_v = [_mm(tri_k[q], stack(vv[q])) for q in quads]

    t_inv = [jnp.where(eye, 1.0, l) for l in l_ab]
    power = [bf(l) for l in l_ab]
    power = [bf(_mm(p, stack(p))) for p in power]
    for _ in range(c.bit_length() - 3):
        both = [_mm(cat(power[q], bf(t_inv[q])), stack(power[q])) for q in quads]
        power = [bf(b[:c]) for b in both]
        t_inv = [t_inv[q] + both[q][c:] for q in quads]
    t_inv = [t_inv[q] + _mm(bf(t_inv[q]), stack(power[q])) for q in quads]

    u = [bf(_mm(bf(t_inv[q]), stack(bf(from_state[q][:c] + from_v[q][:c])))) for q in quads]
    y = [from_state[q][c:] + from_v[q][c:] + _mm(p_rb[q], stack(u[q])) for q in quads]
    upd = [_tn(cat(u[q], vv[q]), cat(bt[q], kt[q])) for q in quads]
    for q in quads:
        g_last = gl_ref[q // nq, 0, :, (q % nq) * QW:(q % nq + 1) * QW]
        s_ref[q] = (state[q] + jnp.where(block_diag, upd[q], 0.0)) * g_last

    e = e_ref[...]
    inv_n = 1.0 / HEAD_DIM
    mean = [_split_dot(t, e) * inv_n for t in y]
    yc = [y[q] - mean[q] for q in quads]
    var = [_split_dot(t * t, e) * inv_n for t in yc]
    for q in quads:
        bi, sl = q // nq, slice((q % nq) * QW, (q % nq + 1) * QW)
        out = (yc[q] * lax.rsqrt(var[q] + GN_EPS) * gg_ref[:, sl] + gb_ref[:, sl]
               + bo_ref[bi, :, sl]) * g_ref[bi, :, sl]
        y_ref[bi, :, sl] = out.astype(y_ref.dtype)


def _wkv(at, rt, bt, kt, v, gl, gate, bonus, gn_gain, gn_bias, e_ones, nb):
    assert CHUNK == HEAD_DIM
    b, s, d = at.shape
    blk = pl.BlockSpec((nb, CHUNK, d), lambda bi, c: (bi, c, 0))
    vec = pl.BlockSpec((1, d), lambda bi, c: (0, 0))
    return pl.pallas_call(
        _wkv_kernel,
        grid=(b // nb, s // CHUNK),
        in_specs=([blk] * 5 + [pl.BlockSpec((nb, 1, 1, d), lambda bi, c: (bi, c, 0, 0))]
                  + [blk] * 2 + [vec] * 2 + [pl.BlockSpec((QW, QW), lambda bi, c: (0, 0))]),
        out_specs=blk,
        out_shape=jax.ShapeDtypeStruct((b, s, d), BF16),
        scratch_shapes=[pltpu.VMEM((nb * d // QW, QW, QW), F32)],
        compiler_params=_cparams("parallel", "arbitrary"),
        name="wkv",
    )(at, rt, bt, kt, v, gl, gate, bonus, gn_gain, gn_bias, e_ones)


def _conv_kernel(g_ref, w_ref, b_ref, lg_ref, lb_ref, y_ref, u_ref, c_ref, sh_ref):
    tc = g_ref.shape[1]
    d = y_ref.shape[2]

    @pl.when(pl.program_id(1) == 0)
    def _():
        u_ref[0:CONV_HALO, :] = jnp.zeros((CONV_HALO, d), F32)

    u_ref[CONV_HALO:CONV_HALO + tc, :] = g_ref[0]

    base = CONV_HALO - (CONV_WIDTH - 1)
    rb = min(tc, CONV_ROWS)
    nt = rb // SUBLANES

    def strip(s, carry):
        cols = pl.ds(pl.multiple_of(s * LANES, LANES), LANES)
        bias = jnp.broadcast_to(b_ref[:, cols], (SUBLANES, LANES))
        for r0 in range(0, tc, rb):
            acc = [bias] * nt
            for res in range(SUBLANES):
                offs = [o for o in range(base, base + CONV_WIDTH) if o % SUBLANES == res]
                lo, hi = offs[0], offs[-1]
                n = hi - lo + rb
                if res:
                    sh_ref[res, 0:n, :] = u_ref[r0 + lo:r0 + lo + n, cols]
                    src, at = sh_ref.at[res], 0
                else:
                    src, at = u_ref.at[:, cols], r0 + lo
                win = [src[at + i * SUBLANES:at + (i + 1) * SUBLANES, :] for i in range(n // SUBLANES)]
                for o in offs:
                    tap = jnp.broadcast_to(w_ref[o - base:o - base + 1, cols], (SUBLANES, LANES))
                    k = (o - lo) // SUBLANES
                    acc = [acc[i] + win[i + k] * tap for i in range(nt)]
            for i in range(nt):
                c_ref[r0 + i * SUBLANES:r0 + (i + 1) * SUBLANES, cols] = acc[i]
        return carry

    lax.fori_loop(0, d // LANES, strip, 0)
    acc = c_ref[...]
    mean = jnp.mean(acc, axis=-1, keepdims=True)
    cen = acc - mean
    var = jnp.mean(cen * cen, axis=-1, keepdims=True)
    cf = cen * lax.rsqrt(var + LN_EPS) * lg_ref[...] + lb_ref[...]
    y_ref[0] = (cf * jax.nn.sigmoid(cf)).astype(y_ref.dtype)
    u_ref[0:CONV_HALO, :] = u_ref[tc:tc + CONV_HALO, :]


def _conv(u_glu, dw_w, dw_b, ln_g, ln_b, tc):
    b, s, d = u_glu.shape
    const = lambda shape: pl.BlockSpec(shape, lambda bi, t: (0, 0))
    return pl.pallas_call(
        _conv_kernel,
        grid=(b, s // tc),
        in_specs=[pl.BlockSpec((1, tc, d), lambda bi, t: (bi, t, 0)),
                  const((CONV_WIDTH, d)), const((1, d)), const((1, d)), const((1, d))],
        out_specs=pl.BlockSpec((1, tc, d), lambda bi, t: (bi, t, 0)),
        out_shape=jax.ShapeDtypeStruct((b, s, d), BF16),
        scratch_shapes=[pltpu.VMEM((tc + CONV_HALO, d), F32), pltpu.VMEM((tc, d), F32),
                        pltpu.VMEM((SUBLANES, min(tc, CONV_ROWS) + CONV_HALO, LANES), F32)],
        compiler_params=_cparams("parallel", "arbitrary"),
        name="conformer_conv",
    )(u_glu, dw_w, dw_b, ln_g, ln_b)


def _outproj_kernel(yr_ref, yc_ref, x_ref, wr_ref, wc_ref, g_ref, o_ref):
    mixed = (jnp.dot(yr_ref[...], wr_ref[...], preferred_element_type=F32)
             + jnp.dot(yc_ref[...], wc_ref[...], preferred_element_type=F32))
    o_ref[...] = x_ref[...] + _rms(mixed, g_ref[...])


def _outproj(y_r, y_c, x2, w_r, w_c, g, tm):
    m, d = x2.shape
    dr, dc = y_r.shape[1], y_c.shape[1]
    return pl.pallas_call(
        _outproj_kernel,
        grid=(m // tm,),
        in_specs=[pl.BlockSpec((tm, dr), lambda i: (i, 0)),
                  pl.BlockSpec((tm, dc), lambda i: (i, 0)),
                  pl.BlockSpec((tm, d), lambda i: (i, 0)),
                  pl.BlockSpec((dr, d), lambda i: (0, 0)),
                  pl.BlockSpec((dc, d), lambda i: (0, 0)),
                  pl.BlockSpec((1, d), lambda i: (0, 0))],
        out_specs=pl.BlockSpec((tm, d), lambda i: (i, 0)),
        out_shape=jax.ShapeDtypeStruct((m, d), F32),
        compiler_params=_cparams("parallel"),
        name="outproj",
    )(y_r, y_c, x2, w_r, w_c, g)


def _mlp_kernel(x_ref, g1_ref, wu_ref, wd_ref, g2_ref, o_ref, h_ref, acc_ref):
    j = pl.program_id(1)

    @pl.when(j == 0)
    def _():
        h_ref[...] = _rms(x_ref[...], g1_ref[...]).astype(BF16)
        acc_ref[...] = jnp.zeros_like(acc_ref)

    a = jnp.maximum(jnp.dot(h_ref[...], wu_ref[...], preferred_element_type=F32), 0.0)
    acc_ref[...] += jnp.dot((a * a).astype(BF16), wd_ref[...], preferred_element_type=F32)

    @pl.when(j == pl.num_programs(1) - 1)
    def _():
        o_ref[...] = x_ref[...] + _rms(acc_ref[...], g2_ref[...])


def _mlp(x2, g1, w_up, w_down, g2, tm, tf):
    m, d = x2.shape
    f = w_up.shape[1]
    return pl.pallas_call(
        _mlp_kernel,
        grid=(m // tm, f // tf),
        in_specs=[pl.BlockSpec((tm, d), lambda i, j: (i, 0)),
                  pl.BlockSpec((1, d), lambda i, j: (0, 0)),
                  pl.BlockSpec((d, tf), lambda i, j: (0, j)),
                  pl.BlockSpec((tf, d), lambda i, j: (j, 0)),
                  pl.BlockSpec((1, d), lambda i, j: (0, 0))],
        out_specs=pl.BlockSpec((tm, d), lambda i, j: (i, 0)),
        out_shape=jax.ShapeDtypeStruct((m, d), F32),
        scratch_shapes=[pltpu.VMEM((tm, d), BF16), pltpu.VMEM((tm, d), F32)],
        compiler_params=_cparams("parallel", "arbitrary"),
        name="mlp",
    )(x2, g1, w_up, w_down, g2)


def _ple_kernel(x_ref, p_ref, wp_ref, g_ref, wg_ref, o_ref):
    x = x_ref[...]
    e = _rms(jnp.dot(p_ref[...].astype(BF16), wp_ref[...], preferred_element_type=F32), g_ref[...])
    gate = jax.nn.sigmoid(jnp.dot(x.astype(BF16), wg_ref[...], preferred_element_type=F32))
    o_ref[...] = x + gate * e


def _ple(x2, p2, w_ple, g, w_gate, tm):
    m, d = x2.shape
    dp = p2.shape[1]
    return pl.pallas_call(
        _ple_kernel,
        grid=(m // tm,),
        in_specs=[pl.BlockSpec((tm, d), lambda i: (i, 0)),
                  pl.BlockSpec((tm, dp), lambda i: (i, 0)),
                  pl.BlockSpec((dp, d), lambda i: (0, 0)),
                  pl.BlockSpec((1, d), lambda i: (0, 0)),
                  pl.BlockSpec((d, d), lambda i: (0, 0))],
        out_specs=pl.BlockSpec((tm, d), lambda i: (i, 0)),
        out_shape=jax.ShapeDtypeStruct((m, d), F32),
        compiler_params=_cparams("parallel"),
        name="ple",
    )(x2, p2, w_ple, g, w_gate)


def _tile(n, want, align):
    t = (min(n, want) // align) * align
    while n % t:
        t -= align
    return t


def _up_weight(w, rows):
    w = jnp.pad(w, ((0, rows - w.shape[0]), (0, 0)))
    hi = w.astype(BF16)
    return jnp.stack([hi, (w - hi.astype(F32)).astype(BF16)])


def _up_weight_packed(w):
    half = LANES // 2
    assert w.shape[0] <= half
    w = jnp.pad(w, ((0, half - w.shape[0]), (0, 0)))
    hi = w.astype(BF16)
    lo = (w - hi.astype(F32)).astype(BF16)
    return jnp.concatenate([hi, hi, lo, jnp.zeros_like(hi)], axis=0)


def kernel(x, p, norm_mix_pre, norm_mix_post, norm_mlp_pre, norm_mlp_post, w_in, w_in_vmix, mu_shift, mu_shift_vmix, w0, w_decay_up, a0, w_iclr_up, v0, w_vmix_up, w_gate_up, k_k, k_a, r_k, gn_gain, gn_bias, dw_w, dw_b, conv_ln_gain, conv_ln_bias, w_out, w_up, w_down, w_ple, norm_ple, w_ple_gate):
    bsz, seq, d_model = x.shape
    depth = w_in.shape[0]
    d_rwkv = w0.shape[1]
    d_conv = dw_w.shape[2]
    m = bsz * seq
    rkv = 3 * d_rwkv
    low = DECAY_RANK + ICLR_RANK + GATE_RANK
    assert d_rwkv % QW == 0 and seq % CHUNK == 0 and d_conv % LANES == 0

    tm = _tile(m, 512, 8)
    tf = _tile(w_up.shape[2], 1024, LANES)
    tp = _tile(seq, 256, CHUNK)
    tc = _tile(seq, 256, CONV_HALO)

    head = jnp.arange(QW) // HEAD_DIM
    e_ones = (head[:, None] == head[None, :]).astype(BF16)
    row = lambda a: a.reshape(1, -1)

    def small_cols(cols, width):
        return jnp.pad(cols, ((0, 0), (0, width - cols.shape[1])))

    x2 = x.reshape(m, d_model)
    v_first = None
    for i in range(depth):
        wi = w_in[i]
        w_rkv = _quad_columns(wi[:, :d_rwkv], wi[:, d_rwkv:2 * d_rwkv], wi[:, 2 * d_rwkv:rkv]).astype(BF16)
        mu_rkv = _quad_columns(row(mu_shift[i, :d_rwkv]), row(mu_shift[i, d_rwkv:2 * d_rwkv]),
                               row(mu_shift[i, 2 * d_rwkv:rkv]))
        w_glu = wi[:, rkv + low:].astype(BF16)
        d0, d1, d2 = rkv, rkv + DECAY_RANK, rkv + DECAY_RANK + ICLR_RANK
        ws = [small_cols(wi[:, d0:d1], SM_ICLR - SM_DECAY),
              small_cols(wi[:, d1:d2], SM_GATE - SM_ICLR),
              small_cols(wi[:, d2:rkv + low], SM_VMIX - SM_GATE)]
        ms = [small_cols(row(mu_shift[i, d0:d1]), SM_ICLR - SM_DECAY),
              small_cols(row(mu_shift[i, d1:d2]), SM_GATE - SM_ICLR),
              small_cols(row(mu_shift[i, d2:rkv + low]), SM_VMIX - SM_GATE)]
        if i == 0:
            ws.append(jnp.zeros((d_model, SM_WIDTH - SM_VMIX), F32))
            ms.append(jnp.zeros((1, SM_WIDTH - SM_VMIX), F32))
            vmix = None
        else:
            ws.append(small_cols(w_in_vmix[i - 1], SM_WIDTH - SM_VMIX))
            ms.append(small_cols(row(mu_shift_vmix[i - 1]), SM_WIDTH - SM_VMIX))
            vmix = (v_first, row(v0[i - 1]), _up_weight_packed(w_vmix_up[i - 1]))
        w_small = jnp.concatenate(ws, axis=1).astype(BF16)
        mu_small = jnp.concatenate(ms, axis=1)

        at, rt, bt, kt, v, gl, gate, bonus, u_glu, *rest = _prep(
            x2.reshape(bsz, seq, d_model), row(norm_mix_pre[i]), w_rkv, w_small, w_glu, mu_rkv, mu_small,
            row(w0[i]), _up_weight_packed(w_decay_up[i]),
            row(a0[i]), _up_weight_packed(w_iclr_up[i]),
            _up_weight(w_gate_up[i], SM_VMIX - SM_GATE),
            row(k_k[i]), row(k_a[i]), row(r_k[i]), e_ones, vmix, tp)
        if i == 0:
            v_first = rest[0]
        y_r = _wkv(at, rt, bt, kt, v, gl, gate, bonus, row(gn_gain[i]), row(gn_bias[i]), e_ones,
                   _tile(bsz, WKV_ROWS, 1))
        y_c = _conv(u_glu, dw_w[i], row(dw_b[i]), row(conv_ln_gain[i]), row(conv_ln_bias[i]), tc)

        wo = w_out[i].astype(BF16)
        x2 = _outproj(y_r.reshape(m, d_rwkv), y_c.reshape(m, d_conv), x2,
                      wo[:d_rwkv], wo[d_rwkv:], row(norm_mix_post[i]), tm)
        x2 = _mlp(x2, row(norm_mlp_pre[i]), w_up[i].astype(BF16), w_down[i].astype(BF16),
                  row(norm_mlp_post[i]), tm, tf)
        x2 = _ple(x2, p[i].reshape(m, -1), w_ple[i].astype(BF16), row(norm_ple[i]),
                  w_ple_gate[i].astype(BF16), tm)
    return x2.reshape(bsz, seq, d_model)
```
